```python
import jax, jax.numpy as jnp
from jax import lax
import numpy as np

D_MODEL = 1024
BATCH = 8
SEQ = 8192
DEPTH = 4

N_MIXERS = 2
N_LRU_LAYERS = (DEPTH + N_MIXERS - 1) // N_MIXERS
N_ATTN_LAYERS = DEPTH // N_MIXERS
N_META = 16
GRID_W = 64
RMS_EPS = 1e-6

D_RNN = D_MODEL
LRU_BLOCKS = 8
LRU_BLOCK_W = D_RNN // LRU_BLOCKS
CONV_W = 4
CONV_PAD_LEFT = 2
LRU_C = 8.0

HEAD_DIM = 128
N_HEADS = D_MODEL // HEAD_DIM
N_KV_HEADS = 2
GQA_GROUP = N_HEADS // N_KV_HEADS
ROPE_AXIS_DIM = HEAD_DIM // 2
ROPE_PAIRS = ROPE_AXIS_DIM // 2
ROPE_THETA = 10000.0
Q_BLOCK = 128
QKV_DIM = (N_HEADS + 2 * N_KV_HEADS) * HEAD_DIM

D_FF = -(-(8 * D_MODEL) // (3 * 256)) * 256

kernel_name = "bidir_hybrid_rglru_axial_gqa_swiglu"


def _rmsnorm(x, g):
    xf = x.astype(jnp.float32)
    y = xf * lax.rsqrt(jnp.mean(xf * xf, axis=-1, keepdims=True) + RMS_EPS)
    return (y * g.astype(jnp.float32)).astype(x.dtype)


def _linear_scan(a, b):
    def combine(left, right):
        a_l, b_l = left
        a_r, b_r = right
        return a_l * a_r, a_r * b_l + b_r
    _, h = lax.associative_scan(combine, (a, b), axis=1)
    return h


def _rglru_direction(u, gate_w, gate_b, lam):
    B, L, _ = u.shape
    ub = u.reshape(B, L, LRU_BLOCKS, LRU_BLOCK_W)
    g = jnp.einsum('blnc,gncd->gblnd', ub, gate_w).reshape(2, B, L, D_RNN)
    g = jax.nn.sigmoid(g.astype(jnp.float32) + gate_b.astype(jnp.float32)[:, None, None, :])
    r, i = g[0], g[1]
    log_a = -LRU_C * r * jax.nn.softplus(-lam.astype(jnp.float32))
    a = jnp.exp(log_a)
    b = jnp.sqrt(-jnp.expm1(2.0 * log_a)) * (i * u.astype(jnp.float32))
    return _linear_scan(a, b)


def _rglru_mixer(h, w_in, conv_w, conv_b, gate_w, gate_b, lam, w_out):
    L = h.shape[1]
    proj = h @ w_in
    y_branch = jax.nn.gelu(proj[..., :D_RNN])
    x_branch = proj[..., D_RNN:]
    xp = jnp.pad(x_branch, ((0, 0), (CONV_PAD_LEFT, CONV_W - 1 - CONV_PAD_LEFT), (0, 0)))
    xc = conv_b
    for k in range(CONV_W):
        xc = xc + xp[:, k:k + L] * conv_w[k]
    fwd = _rglru_direction(xc, gate_w[0], gate_b[0], lam[0])
    bwd = jnp.flip(_rglru_direction(jnp.flip(xc, axis=1), gate_w[1], gate_b[1], lam[1]), axis=1)
    rec = (fwd + bwd).astype(h.dtype)
    return (rec * y_branch) @ w_out


def _axial_rope_angles(n_tokens):
    rows = n_tokens // GRID_W
    inv_freq = ROPE_THETA ** (-jnp.arange(0, ROPE_AXIS_DIM, 2, dtype=jnp.float32) / ROPE_AXIS_DIM)
    ang_r = jnp.arange(rows, dtype=jnp.float32)[:, None] * inv_freq
    ang_c = jnp.arange(GRID_W, dtype=jnp.float32)[:, None] * inv_freq
    ang = jnp.stack([jnp.broadcast_to(ang_r[:, None, :], (rows, GRID_W, ROPE_PAIRS)),
                     jnp.broadcast_to(ang_c[None, :, :], (rows, GRID_W, ROPE_PAIRS))], axis=2)
    ang = ang.reshape(rows * GRID_W, 2, ROPE_PAIRS)
    ang = jnp.concatenate([jnp.zeros((N_META, 2, ROPE_PAIRS), jnp.float32), ang], axis=0)
    return jnp.cos(ang), jnp.sin(ang)


def _apply_rope(x, cos, sin):
    xs = x.reshape(*x.shape[:-1], 2, 2, ROPE_PAIRS)
    x1, x2 = xs[..., 0, :], xs[..., 1, :]
    c = cos.astype(x.dtype)[None, :, None]
    s = sin.astype(x.dtype)[None, :, None]
    out = jnp.stack([x1 * c - x2 * s, x2 * c + x1 * s], axis=-2)
    return out.reshape(x.shape)


def _attn_block(qb, k, v):
    s = jnp.einsum('bqkgd,bskd->bkgqs', qb, k).astype(jnp.float32)
    p = jax.nn.softmax(s, axis=-1).astype(v.dtype)
    return jnp.einsum('bkgqs,bskd->bqkgd', p, v)


def _attention_mixer(h, w_qkv, q_gain, k_gain, w_out, cos, sin):
    B, L, _ = h.shape
    qkv = h @ w_qkv
    q = qkv[..., :N_HEADS * HEAD_DIM].reshape(B, L, N_HEADS, HEAD_DIM)
    k = qkv[..., N_HEADS * HEAD_DIM:(N_HEADS + N_KV_HEADS) * HEAD_DIM].reshape(B, L, N_KV_HEADS, HEAD_DIM)
    v = qkv[..., (N_HEADS + N_KV_HEADS) * HEAD_DIM:].reshape(B, L, N_KV_HEADS, HEAD_DIM)
    q = _apply_rope(_rmsnorm(q, q_gain), cos, sin) * (HEAD_DIM ** -0.5)
    k = _apply_rope(_rmsnorm(k, k_gain), cos, sin)
    q = q.reshape(B, L, N_KV_HEADS, GQA_GROUP, HEAD_DIM)
    out_meta = _attn_block(q[:, :N_META], k, v)
    n_blk = (L - N_META) // Q_BLOCK
    q_real = q[:, N_META:].reshape(B, n_blk, Q_BLOCK, N_KV_HEADS, GQA_GROUP, HEAD_DIM)
    q_real = q_real.transpose(1, 0, 2, 3, 4, 5)
    out_real = lax.map(lambda qb: _attn_block(qb, k, v), q_real)
    out_real = out_real.transpose(1, 0, 2, 3, 4, 5).reshape(B, L - N_META, N_KV_HEADS, GQA_GROUP, HEAD_DIM)
    o = jnp.concatenate([out_meta, out_real], axis=1).reshape(B, L, N_HEADS * HEAD_DIM)
    return o @ w_out


def _swiglu(h, w_in, w_out):
    gu = h @ w_in
    return (jax.nn.silu(gu[..., :D_FF]) * gu[..., D_FF:]) @ w_out


def setup_inputs(seed: int = 0) -> dict:
    key = jax.random.key(seed)
    ks = jax.random.split(key, 18)
    f32 = jnp.float32
    nrm = lambda k, shape, fan_in: jax.random.normal(k, shape, f32) * (fan_in ** -0.5)
    x = jax.random.normal(ks[0], (BATCH, SEQ, D_MODEL), f32)
    meta_tokens = jax.random.normal(ks[1], (N_META, D_MODEL), f32)
    norm_gains = 1.0 + 0.02 * jax.random.normal(ks[2], (DEPTH, 4, D_MODEL), f32)
    lru_w_in = nrm(ks[3], (N_LRU_LAYERS, D_MODEL, 2 * D_RNN), D_MODEL)
    lru_conv_w = nrm(ks[4], (N_LRU_LAYERS, CONV_W, D_RNN), CONV_W)
    lru_conv_b = 0.01 * jax.random.normal(ks[5], (N_LRU_LAYERS, D_RNN), f32)
    lru_gate_w = nrm(ks[6], (N_LRU_LAYERS, 2, 2, LRU_BLOCKS, LRU_BLOCK_W, LRU_BLOCK_W), LRU_BLOCK_W)
    lru_gate_b = 0.01 * jax.random.normal(ks[7], (N_LRU_LAYERS, 2, 2, D_RNN), f32)
    a_c = jax.random.uniform(ks[8], (N_LRU_LAYERS, 2, D_RNN), f32, 0.9, 0.999)
    s = a_c ** (1.0 / LRU_C)
    lru_lambda = jnp.log(s) - jnp.log1p(-s)
    lru_w_out = nrm(ks[9], (N_LRU_LAYERS, D_RNN, D_MODEL), D_RNN)
    attn_w_qkv = nrm(ks[10], (N_ATTN_LAYERS, D_MODEL, QKV_DIM), D_MODEL)
    attn_q_gain = 1.0 + 0.02 * jax.random.normal(ks[11], (N_ATTN_LAYERS, HEAD_DIM), f32)
    attn_k_gain = 1.0 + 0.02 * jax.random.normal(ks[12], (N_ATTN_LAYERS, HEAD_DIM), f32)
    attn_w_out = nrm(ks[13], (N_ATTN_LAYERS, N_HEADS * HEAD_DIM, D_MODEL), N_HEADS * HEAD_DIM)
    ffn_w_in = nrm(ks[14], (DEPTH, D_MODEL, 2 * D_FF), D_MODEL)
    ffn_w_out = nrm(ks[15], (DEPTH, D_FF, D_MODEL), D_FF)
    return {"x": x, "meta_tokens": meta_tokens, "norm_gains": norm_gains,
            "lru_w_in": lru_w_in, "lru_conv_w": lru_conv_w, "lru_conv_b": lru_conv_b,
            "lru_gate_w": lru_gate_w, "lru_gate_b": lru_gate_b, "lru_lambda": lru_lambda,
            "lru_w_out": lru_w_out, "attn_w_qkv": attn_w_qkv, "attn_q_gain": attn_q_gain,
            "attn_k_gain": attn_k_gain, "attn_w_out": attn_w_out,
            "ffn_w_in": ffn_w_in, "ffn_w_out": ffn_w_out}


def reference(x, meta_tokens, norm_gains, lru_w_in, lru_conv_w, lru_conv_b, lru_gate_w,
              lru_gate_b, lru_lambda, lru_w_out, attn_w_qkv, attn_q_gain, attn_k_gain,
              attn_w_out, ffn_w_in, ffn_w_out):
    B, S, _ = x.shape
    meta = jnp.broadcast_to(meta_tokens.astype(x.dtype)[None], (B, N_META, D_MODEL))
    h = jnp.concatenate([meta, x], axis=1)
    cos, sin = _axial_rope_angles(S)
    for layer in range(DEPTH):
        g = norm_gains[layer]
        slot = layer // N_MIXERS
        u = _rmsnorm(h, g[0])
        if layer % N_MIXERS == 0:
            m = _rglru_mixer(u, lru_w_in[slot], lru_conv_w[slot], lru_conv_b[slot],
                             lru_gate_w[slot], lru_gate_b[slot], lru_lambda[slot], lru_w_out[slot])
        else:
            m = _attention_mixer(u, attn_w_qkv[slot], attn_q_gain[slot], attn_k_gain[slot],
                                 attn_w_out[slot], cos, sin)
        h = h + _rmsnorm(m, g[1])
        u = _rmsnorm(h, g[2])
        h = h + _rmsnorm(_swiglu(u, ffn_w_in[layer], ffn_w_out[layer]), g[3])
    return h[:, N_META:]
```

```python
import functools

import jax
import jax.numpy as jnp
from jax import lax
from jax.experimental import pallas as pl
from jax.experimental.pallas import tpu as pltpu

F32 = jnp.float32
BF16 = jnp.bfloat16

D_MODEL = 1024
N_META = 16
GRID_W = 64
RMS_EPS = 1e-6
LRU_BLOCKS = 8
LRU_BLOCK_W = 128
LRU_C = 8.0
HEAD_DIM = 128
N_HEADS = 8
N_KV_HEADS = 2
GQA_GROUP = 4
ROPE_PAIRS = 32
ROPE_THETA = 10000.0

VMEM_LIMIT_BYTES = 56 * 1024 * 1024

TOKEN_TILE = 912
SCAN_TILE = 432
Q_TILE = 432
KEY_CHUNK = 1024
FF_CHUNK = 1408
HALO = 16


def _params(*sem):
    return pltpu.CompilerParams(dimension_semantics=sem, vmem_limit_bytes=VMEM_LIMIT_BYTES)


def _tok_spec(tile, width):
    return pl.BlockSpec((None, tile, width), lambda b, i: (b, i, 0))


def _const_spec(shape):
    zeros = (0,) * len(shape)
    return pl.BlockSpec(shape, lambda b, i: zeros, pipeline_mode=pl.Buffered(1))


def _rms(x, g):
    return x * lax.rsqrt(jnp.mean(x * x, axis=-1, keepdims=True) + RMS_EPS) * g


def _sigmoid(x):
    return 0.5 * jnp.tanh(0.5 * x) + 0.5


def _gelu_tanh(x):
    return 0.5 * x * (1.0 + jnp.tanh(0.7978845608028654 * (x + 0.044715 * (x * x * x))))


def _lru_in_kernel(h_ref, g_ref, w_ref, y_ref, xb_ref):
    u = _rms(h_ref[...], g_ref[...]).astype(BF16)
    d = y_ref.shape[-1]
    y_ref[...] = _gelu_tanh(jnp.dot(u, w_ref[:, :d], preferred_element_type=F32)).astype(BF16)
    xb_ref[...] = jnp.dot(u, w_ref[:, d:], preferred_element_type=F32).astype(BF16)


def _lru_in(h, g, w):
    B, L, D = h.shape
    out = jax.ShapeDtypeStruct((B, L, D), BF16)
    return pl.pallas_call(
        _lru_in_kernel,
        out_shape=(out, out),
        grid=(B, L // TOKEN_TILE),
        in_specs=[_tok_spec(TOKEN_TILE, D), _const_spec((1, D)), _const_spec((D, 2 * D))],
        out_specs=(_tok_spec(TOKEN_TILE, D), _tok_spec(TOKEN_TILE, D)),
        compiler_params=_params("parallel", "parallel"),
        name="lru_in",
    )(h, g, w)


def _conv_and_gates(c, n_chunks, xprev_ref, xcur_ref, xnext_ref, cw_ref, cb_ref, gw_ref,
                    gb_ref, lam_ref, xpad_scr, a_scr, b_scr):
    tl = xcur_ref.shape[0]
    xpad_scr[pl.ds(8, tl), :] = xcur_ref[...].astype(F32)
    prev = xprev_ref[...].astype(F32)
    xpad_scr[pl.ds(0, 8), :] = jnp.where(c > 0, prev[HALO - 8:, :], 0.0)
    nxt = xnext_ref[...].astype(F32)
    xpad_scr[pl.ds(8 + tl, 8), :] = jnp.where(c < n_chunks - 1, nxt[:8, :], 0.0)
    xc = (cb_ref[...]
          + xpad_scr[pl.ds(6, tl), :] * cw_ref[0:1, :]
          + xpad_scr[pl.ds(7, tl), :] * cw_ref[1:2, :]
          + xpad_scr[pl.ds(8, tl), :] * cw_ref[2:3, :]
          + xpad_scr[pl.ds(9, tl), :] * cw_ref[3:4, :])
    xcb = xc.astype(BF16)
    x = -lam_ref[...]
    e = jnp.exp(-jnp.abs(x))
    u1 = 1.0 + e
    log1p_e = jnp.where(u1 == 1.0, e, jnp.log(u1) * (e / (u1 - 1.0)))
    neg_c_softplus = -LRU_C * (jnp.maximum(x, 0.0) + log1p_e)
    w = LRU_BLOCK_W
    for n in range(LRU_BLOCKS):
        sl = slice(n * w, (n + 1) * w)
        gp = jnp.dot(xcb[:, sl], gw_ref[n], preferred_element_type=F32)
        r = _sigmoid(gp[:, :w] + gb_ref[0:1, sl])
        i = _sigmoid(gp[:, w:] + gb_ref[1:2, sl])
        log_a = r * neg_c_softplus[:, sl]
        a = jnp.exp(log_a)
        z = 2.0 * log_a
        series = z * (1.0 + z * (0.5 + z * (1.0 / 6.0 + z * (1.0 / 24.0 + z * (1.0 / 120.0)))))
        expm1_z = jnp.where(z > -0.05, series, a * a - 1.0)
        a_scr[:, sl] = a
        b_scr[:, sl] = jnp.sqrt(-expm1_z) * (i * xc[:, sl])


def _scan8(a, b, rows, reverse):
    for s in (1, 2, 4):
        if reverse:
            a_sh = pltpu.roll(a, 8 - s, 0)
            b_sh = pltpu.roll(b, 8 - s, 0)
            m = rows < 8 - s
        else:
            a_sh = pltpu.roll(a, s, 0)
            b_sh = pltpu.roll(b, s, 0)
            m = rows >= s
        b = jnp.where(m, a * b_sh + b, b)
        a = jnp.where(m, a * a_sh, a)
    return a, b


def _scan16(j, carry, a_scr, b_scr, rows, reverse):
    r0 = pl.multiple_of(j * 16, 16)
    a0, b0 = _scan8(a_scr[pl.ds(r0, 8), :], b_scr[pl.ds(r0, 8), :], rows, reverse)
    a1, b1 = _scan8(a_scr[pl.ds(r0 + 8, 8), :], b_scr[pl.ds(r0 + 8, 8), :], rows, reverse)
    if reverse:
        h1 = a1 * carry + b1
        h0 = a0 * h1[0:1, :] + b0
        carry = h0[0:1, :]
    else:
        h0 = a0 * carry + b0
        h1 = a1 * h0[7:8, :] + b1
        carry = h1[7:8, :]
    return r0, jnp.concatenate([h0, h1], axis=0), carry


def _lru_fwd_kernel(xprev_ref, xcur_ref, xnext_ref, cw_ref, cb_ref, gw_ref, gb_ref, lam_ref,
                    hf_ref, xpad_scr, a_scr, b_scr, carry_scr):
    c = pl.program_id(1)
    n_chunks = pl.num_programs(1)
    tl, d = xcur_ref.shape

    @pl.when(c == 0)
    def _():
        carry_scr[...] = jnp.zeros_like(carry_scr)

    _conv_and_gates(c, n_chunks, xprev_ref, xcur_ref, xnext_ref, cw_ref, cb_ref, gw_ref, gb_ref,
                    lam_ref, xpad_scr, a_scr, b_scr)
    rows = lax.broadcasted_iota(jnp.int32, (8, d), 0)

    def body(j, carry):
        r0, h, carry = _scan16(j, carry, a_scr, b_scr, rows, reverse=False)
        hf_ref[pl.ds(r0, 16), :] = h.astype(BF16)
        return carry

    carry_scr[...] = lax.fori_loop(0, tl // 16, body, carry_scr[...])


def _lru_bwd_kernel(xprev_ref, xcur_ref, xnext_ref, cw_ref, cb_ref, gw_ref, gb_ref, lam_ref,
                    hf_ref, y_ref, h_ref, wout_ref, g_ref,
                    out_ref, xpad_scr, a_scr, b_scr, carry_scr, z_scr):
    i = pl.program_id(1)
    n_chunks = pl.num_programs(1)
    c = n_chunks - 1 - i
    tl, d = xcur_ref.shape

    @pl.when(i == 0)
    def _():
        carry_scr[...] = jnp.zeros_like(carry_scr)

    _conv_and_gates(c, n_chunks, xprev_ref, xcur_ref, xnext_ref, cw_ref, cb_ref, gw_ref, gb_ref,
                    lam_ref, xpad_scr, a_scr, b_scr)
    rows = lax.broadcasted_iota(jnp.int32, (8, d), 0)
    n_tiles = tl // 16

    def body(jj, carry):
        r0, hb, carry = _scan16(n_tiles - 1 - jj, carry, a_scr, b_scr, rows, reverse=True)
        rec = hf_ref[pl.ds(r0, 16), :].astype(F32) + hb
        z_scr[pl.ds(r0, 16), :] = (rec * y_ref[pl.ds(r0, 16), :].astype(F32)).astype(BF16)
        return carry

    carry_scr[...] = lax.fori_loop(0, n_tiles, body, carry_scr[...])
    m = jnp.dot(z_scr[...], wout_ref[...], preferred_element_type=F32)
    out_ref[...] = h_ref[...] + _rms(m, g_ref[...])


def _scan_in_specs(L, D, chunk_of):
    tl = SCAN_TILE
    per = tl // HALO
    last = L // HALO - 1
    cur = pl.BlockSpec((None, tl, D), lambda b, i: (b, chunk_of(i), 0))
    prev = pl.BlockSpec((None, HALO, D), lambda b, i: (b, jnp.maximum(chunk_of(i) * per - 1, 0), 0))
    nxt = pl.BlockSpec((None, HALO, D), lambda b, i: (b, jnp.minimum((chunk_of(i) + 1) * per, last), 0))
    consts = [_const_spec((4, D)), _const_spec((1, D)),
              _const_spec((LRU_BLOCKS, LRU_BLOCK_W, 2 * LRU_BLOCK_W)), _const_spec((2, D)),
              _const_spec((1, D))]
    return cur, [prev, cur, nxt] + consts


def _scan_scratch(D):
    tl = SCAN_TILE
    return [pltpu.VMEM((tl + 16, D), F32), pltpu.VMEM((tl, D), F32), pltpu.VMEM((tl, D), F32),
            pltpu.VMEM((1, D), F32)]


def _lru_fwd(xb, cw, cb, gw, gb, lam):
    B, L, D = xb.shape
    cur, in_specs = _scan_in_specs(L, D, lambda i: i)
    return pl.pallas_call(
        _lru_fwd_kernel,
        out_shape=jax.ShapeDtypeStruct((B, L, D), BF16),
        grid=(B, L // SCAN_TILE),
        in_specs=in_specs,
        out_specs=cur,
        scratch_shapes=_scan_scratch(D),
        compiler_params=_params("parallel", "arbitrary"),
        name="lru_fwd",
    )(xb, xb, xb, cw, cb, gw, gb, lam)


def _lru_bwd(xb, cw, cb, gw, gb, lam, hf, y, h, w_out, g):
    B, L, D = xb.shape
    n_chunks = L // SCAN_TILE
    cur, in_specs = _scan_in_specs(L, D, lambda i: n_chunks - 1 - i)
    in_specs = in_specs + [cur, cur, cur, _const_spec((D, D)), _const_spec((1, D))]
    return pl.pallas_call(
        _lru_bwd_kernel,
        out_shape=jax.ShapeDtypeStruct((B, L, D), F32),
        grid=(B, n_chunks),
        in_specs=in_specs,
        out_specs=cur,
        scratch_shapes=_scan_scratch(D) + [pltpu.VMEM((SCAN_TILE, D), BF16)],
        compiler_params=_params("parallel", "arbitrary"),
        name="lru_bwd",
    )(xb, xb, xb, cw, cb, gw, gb, lam, hf, y, h, w_out, g)


def _rope(x, cos, sin_signed, first_half):
    partner = jnp.where(first_half, pltpu.roll(x, 96, 1), pltpu.roll(x, 32, 1))
    return x * cos + partner * sin_signed


def _qkv_kernel(h_ref, g_ref, w_ref, qg_ref, kg_ref, cos_ref, sin_ref, q_ref, k_ref, v_ref):
    u = _rms(h_ref[...], g_ref[...]).astype(BF16)
    hd = HEAD_DIM
    nq = N_HEADS * hd
    nk = N_KV_HEADS * hd
    cos = cos_ref[...]
    sin = sin_ref[...]
    first_half = (lax.broadcasted_iota(jnp.int32, cos.shape, 1) % 64) < 32
    q = jnp.dot(u, w_ref[:, :nq], preferred_element_type=F32)
    for hh in range(N_HEADS):
        sl = slice(hh * hd, (hh + 1) * hd)
        qh = _rope(_rms(q[:, sl], qg_ref[...]), cos, sin, first_half) * (HEAD_DIM ** -0.5)
        q_ref[:, sl] = qh.astype(BF16)
    kv = jnp.dot(u, w_ref[:, nq:], preferred_element_type=F32)
    for hh in range(N_KV_HEADS):
        sl = slice(hh * hd, (hh + 1) * hd)
        k_ref[:, sl] = _rope(_rms(kv[:, sl], kg_ref[...]), cos, sin, first_half).astype(BF16)
    v_ref[...] = kv[:, nk:].astype(BF16)


def _qkv(h, g, w, qg, kg, cos, sin):
    B, L, D = h.shape
    nq = N_HEADS * HEAD_DIM
    nk = N_KV_HEADS * HEAD_DIM
    tab = pl.BlockSpec((TOKEN_TILE, HEAD_DIM), lambda b, i: (i, 0))
    return pl.pallas_call(
        _qkv_kernel,
        out_shape=(jax.ShapeDtypeStruct((B, L, nq), BF16), jax.ShapeDtypeStruct((B, L, nk), BF16),
                   jax.ShapeDtypeStruct((B, L, nk), BF16)),
        grid=(B, L // TOKEN_TILE),
        in_specs=[_tok_spec(TOKEN_TILE, D), _const_spec((1, D)), _const_spec((D, nq + 2 * nk)),
                  _const_spec((1, HEAD_DIM)), _const_spec((1, HEAD_DIM)), tab, tab],
        out_specs=(_tok_spec(TOKEN_TILE, nq), _tok_spec(TOKEN_TILE, nk), _tok_spec(TOKEN_TILE, nk)),
        compiler_params=_params("parallel", "parallel"),
        name="qkv",
    )(h, g, w, qg, kg, cos, sin)


def _attn_kernel(q_ref, k_ref, v_ref, o_ref):
    tq = q_ref.shape[0]
    L = k_ref.shape[0]
    hd = HEAD_DIM
    q = jnp.concatenate([q_ref[:, g * hd:(g + 1) * hd] for g in range(GQA_GROUP)], axis=0)
    rows = GQA_GROUP * tq

    def step(kc, vc, carry):
        m, l, acc = carry
        s = lax.dot_general(q, kc, (((1,), (1,)), ((), ())), preferred_element_type=F32)
        m_new = jnp.maximum(m, jnp.max(s, axis=-1, keepdims=True))
        alpha = jnp.exp(m - m_new)
        p = jnp.exp(s - m_new)
        l = alpha * l + jnp.sum(p, axis=-1, keepdims=True)
        acc = alpha * acc + jnp.dot(p.astype(BF16), vc, preferred_element_type=F32)
        return m_new, l, acc

    n_full = L // KEY_CHUNK
    tail = L - n_full * KEY_CHUNK
    carry = (jnp.full((rows, 1), -jnp.inf, F32), jnp.zeros((rows, 1), F32), jnp.zeros((rows, hd), F32))
    if tail:
        carry = step(k_ref[pl.ds(n_full * KEY_CHUNK, tail), :], v_ref[pl.ds(n_full * KEY_CHUNK, tail), :],
                     carry)

    def body(j, carry):
        k0 = pl.multiple_of(j * KEY_CHUNK, KEY_CHUNK)
        return step(k_ref[pl.ds(k0, KEY_CHUNK), :], v_ref[pl.ds(k0, KEY_CHUNK), :], carry)

    m, l, acc = lax.fori_loop(0, n_full, body, carry)
    o = acc * (1.0 / l)
    for g in range(GQA_GROUP):
        o_ref[:, g * hd:(g + 1) * hd] = o[g * tq:(g + 1) * tq, :].astype(BF16)


def _attention(q, k, v):
    B, L, nq = q.shape
    gw = GQA_GROUP * HEAD_DIM
    q_spec = pl.BlockSpec((None, Q_TILE, gw), lambda b, kh, i: (b, i, kh))
    kv_spec = pl.BlockSpec((None, L, HEAD_DIM), lambda b, kh, i: (b, 0, kh))
    return pl.pallas_call(
        _attn_kernel,
        out_shape=jax.ShapeDtypeStruct((B, L, nq), BF16),
        grid=(B, N_KV_HEADS, L // Q_TILE),
        in_specs=[q_spec, kv_spec, kv_spec],
        out_specs=q_spec,
        compiler_params=_params("parallel", "parallel", "parallel"),
        name="attention",
    )(q, k, v)


def _outproj_kernel(z_ref, h_ref, w_ref, g_ref, out_ref):
    m = jnp.dot(z_ref[...], w_ref[...], preferred_element_type=F32)
    out_ref[...] = h_ref[...] + _rms(m, g_ref[...])


def _outproj(z, h, w, g):
    B, L, D = h.shape
    return pl.pallas_call(
        _outproj_kernel,
        out_shape=jax.ShapeDtypeStruct((B, L, D), F32),
        grid=(B, L // TOKEN_TILE),
        in_specs=[_tok_spec(TOKEN_TILE, z.shape[-1]), _tok_spec(TOKEN_TILE, D),
                  _const_spec(w.shape), _const_spec((1, D))],
        out_specs=_tok_spec(TOKEN_TILE, D),
        compiler_params=_params("parallel", "parallel"),
        name="attn_out",
    )(z, h, w, g)


def _ffn_kernel(h_ref, g_pre_ref, w_in_ref, w_out_ref, g_post_ref, out_ref):
    h = h_ref[...]
    u = _rms(h, g_pre_ref[...]).astype(BF16)
    d_ff = w_out_ref.shape[0]
    acc = None
    for c0 in range(0, d_ff, FF_CHUNK):
        gate = jnp.dot(u, w_in_ref[:, c0:c0 + FF_CHUNK], preferred_element_type=F32)
        up = jnp.dot(u, w_in_ref[:, d_ff + c0:d_ff + c0 + FF_CHUNK], preferred_element_type=F32)
        act = (gate * _sigmoid(gate) * up).astype(BF16)
        part = jnp.dot(act, w_out_ref[c0:c0 + FF_CHUNK, :], preferred_element_type=F32)
        acc = part if acc is None else acc + part
    out_ref[...] = h + _rms(acc, g_post_ref[...])


def _ffn(h, g_pre, w_in, w_out, g_post):
    B, L, D = h.shape
    return pl.pallas_call(
        _ffn_kernel,
        out_shape=jax.ShapeDtypeStruct((B, L, D), F32),
        grid=(B, L // TOKEN_TILE),
        in_specs=[_tok_spec(TOKEN_TILE, D), _const_spec((1, D)), _const_spec(w_in.shape),
                  _const_spec(w_out.shape), _const_spec((1, D))],
        out_specs=_tok_spec(TOKEN_TILE, D),
        compiler_params=_params("parallel", "parallel"),
        name="ffn",
    )(h, g_pre, w_in, w_out, g_post)


def _rope_tables(n_tokens):
    rows = n_tokens // GRID_W
    inv_freq = ROPE_THETA ** (-jnp.arange(0, 2 * ROPE_PAIRS, 2, dtype=F32) / (2 * ROPE_PAIRS))
    ang_r = jnp.arange(rows, dtype=F32)[:, None] * inv_freq
    ang_c = jnp.arange(GRID_W, dtype=F32)[:, None] * inv_freq
    ang_r = jnp.broadcast_to(ang_r[:, None, :], (rows, GRID_W, ROPE_PAIRS)).reshape(n_tokens, ROPE_PAIRS)
    ang_c = jnp.broadcast_to(ang_c[None, :, :], (rows, GRID_W, ROPE_PAIRS)).reshape(n_tokens, ROPE_PAIRS)
    ang = jnp.concatenate([ang_r, ang_r, ang_c, ang_c], axis=-1)
    ang = jnp.concatenate([jnp.zeros((N_META, HEAD_DIM), F32), ang], axis=0)
    sign = jnp.tile(jnp.concatenate([-jnp.ones((ROPE_PAIRS,), F32), jnp.ones((ROPE_PAIRS,), F32)]), 2)
    return jnp.cos(ang), jnp.sin(ang) * sign


def kernel(x, meta_tokens, norm_gains, lru_w_in, lru_conv_w, lru_conv_b, lru_gate_w, lru_gate_b,
           lru_lambda, lru_w_out, attn_w_qkv, attn_q_gain, attn_k_gain, attn_w_out, ffn_w_in,
           ffn_w_out):
    B, S, D = x.shape
    depth = norm_gains.shape[0]
    meta = jnp.broadcast_to(meta_tokens.astype(x.dtype)[None], (B, N_META, D))
    h = jnp.concatenate([meta, x], axis=1)
    cos, sin = _rope_tables(S)
    gains = norm_gains.reshape(depth, 4, 1, D)
    for layer in range(depth):
        g = gains[layer]
        slot = layer // 2
        if layer % 2 == 0:
            y, xb = _lru_in(h, g[0], lru_w_in[slot].astype(BF16))
            cw = lru_conv_w[slot]
            cb = lru_conv_b[slot].reshape(1, D)
            gw = jnp.concatenate([lru_gate_w[slot][:, 0], lru_gate_w[slot][:, 1]], axis=-1).astype(BF16)
            gb = lru_gate_b[slot]
            lam = lru_lambda[slot].reshape(2, 1, D)
            hf = _lru_fwd(xb, cw, cb, gw[0], gb[0], lam[0])
            h = _lru_bwd(xb, cw, cb, gw[1], gb[1], lam[1], hf, y, h, lru_w_out[slot].astype(BF16), g[1])
        else:
            q, k, v = _qkv(h, g[0], attn_w_qkv[slot].astype(BF16), attn_q_gain[slot].reshape(1, HEAD_DIM),
                           attn_k_gain[slot].reshape(1, HEAD_DIM), cos, sin)
            o = _attention(q, k, v)
            h = _outproj(o, h, attn_w_out[slot].astype(BF16), g[1])
        h = _ffn(h, g[2], ffn_w_in[layer].astype(BF16), ffn_w_out[layer].astype(BF16), g[3])
    return h[:, N_META:]
```

```python
import functools

import jax
import jax.numpy as jnp
from jax import lax
from jax.experimental import pallas as pl
from jax.experimental.pallas import tpu as pltpu

F32 = jnp.float32
BF16 = jnp.bfloat16

D_MODEL = 1024
N_META = 16
GRID_W = 64
RMS_EPS = 1e-6
LRU_BLOCKS = 8
LRU_BLOCK_W = 128
LRU_C = 8.0
HEAD_DIM = 128
N_HEADS = 8
N_KV_HEADS = 2
GQA_GROUP = 4
ROPE_PAIRS = 32
ROPE_THETA = 10000.0

VMEM_LIMIT_BYTES = 56 * 1024 * 1024

TOKEN_TILE = 912
SCAN_TILE = 432
Q_TILE = 432
KEY_CHUNK = 1024
ATTN_UNROLL = 2
FF_CHUNK = 1408
Q_SCALE = HEAD_DIM ** -0.5 * 1.4426950408889634
SCORE_BOUND_MAX = 80.0
HALO = 16


def _params(*sem):
    return pltpu.CompilerParams(dimension_semantics=sem, vmem_limit_bytes=VMEM_LIMIT_BYTES)


def _tok_spec(tile, width):
    return pl.BlockSpec((None, tile, width), lambda b, i: (b, i, 0))


def _const_spec(shape):
    zeros = (0,) * len(shape)
    return pl.BlockSpec(shape, lambda b, i: zeros, pipeline_mode=pl.Buffered(1))


def _rms(x, g):
    return x * lax.rsqrt(jnp.mean(x * x, axis=-1, keepdims=True) + RMS_EPS) * g


def _sigmoid(x):
    return 0.5 * jnp.tanh(0.5 * x) + 0.5


def _gelu_tanh(x):
    return 0.5 * x * (1.0 + jnp.tanh(0.7978845608028654 * (x + 0.044715 * (x * x * x))))


def _lru_in_kernel(h_ref, g_ref, w_ref, y_ref, xb_ref):
    u = _rms(h_ref[...], g_ref[...]).astype(BF16)
    d = y_ref.shape[-1]
    y_ref[...] = _gelu_tanh(jnp.dot(u, w_ref[:, :d], preferred_element_type=F32)).astype(BF16)
    xb_ref[...] = jnp.dot(u, w_ref[:, d:], preferred_element_type=F32).astype(BF16)


def _lru_in(h, g, w):
    B, L, D = h.shape
    out = jax.ShapeDtypeStruct((B, L, D), BF16)
    return pl.pallas_call(
        _lru_in_kernel,
        out_shape=(out, out),
        grid=(B, L // TOKEN_TILE),
        in_specs=[_tok_spec(TOKEN_TILE, D), _const_spec((1, D)), _const_spec((D, 2 * D))],
        out_specs=(_tok_spec(TOKEN_TILE, D), _tok_spec(TOKEN_TILE, D)),
        compiler_params=_params("parallel", "parallel"),
        name="lru_in",
    )(h, g, w)


def _conv_and_gates(c, n_chunks, xprev_ref, xcur_ref, xnext_ref, cw_ref, cb_ref, gw_ref,
                    gb_ref, lam_ref, xpad_scr, a_scr, b_scr):
    tl = xcur_ref.shape[0]
    xpad_scr[pl.ds(8, tl), :] = xcur_ref[...].astype(F32)
    prev = xprev_ref[...].astype(F32)
    xpad_scr[pl.ds(0, 8), :] = jnp.where(c > 0, prev[HALO - 8:, :], 0.0)
    nxt = xnext_ref[...].astype(F32)
    xpad_scr[pl.ds(8 + tl, 8), :] = jnp.where(c < n_chunks - 1, nxt[:8, :], 0.0)
    xc = (cb_ref[...]
          + xpad_scr[pl.ds(6, tl), :] * cw_ref[0:1, :]
          + xpad_scr[pl.ds(7, tl), :] * cw_ref[1:2, :]
          + xpad_scr[pl.ds(8, tl), :] * cw_ref[2:3, :]
          + xpad_scr[pl.ds(9, tl), :] * cw_ref[3:4, :])
    xcb = xc.astype(BF16)
    x = -lam_ref[...]
    e = jnp.exp(-jnp.abs(x))
    u1 = 1.0 + e
    log1p_e = jnp.where(u1 == 1.0, e, jnp.log(u1) * (e / (u1 - 1.0)))
    neg_c_softplus = -LRU_C * (jnp.maximum(x, 0.0) + log1p_e)
    w = LRU_BLOCK_W
    for n in range(LRU_BLOCKS):
        sl = slice(n * w, (n + 1) * w)
        gp = jnp.dot(xcb[:, sl], gw_ref[n], preferred_element_type=F32)
        r = _sigmoid(gp[:, :w] + gb_ref[0:1, sl])
        i = _sigmoid(gp[:, w:] + gb_ref[1:2, sl])
        log_a = r * neg_c_softplus[:, sl]
        a = jnp.exp(log_a)
        z = 2.0 * log_a
        series = z * (1.0 + z * (0.5 + z * (1.0 / 6.0 + z * (1.0 / 24.0 + z * (1.0 / 120.0)))))
        expm1_z = jnp.where(z > -0.05, series, a * a - 1.0)
        a_scr[:, sl] = a
        b_scr[:, sl] = jnp.sqrt(-expm1_z) * (i * xc[:, sl])


def _scan8(a, b, rows, reverse):
    for s in (1, 2, 4):
        if reverse:
            a_sh = pltpu.roll(a, 8 - s, 0)
            b_sh = pltpu.roll(b, 8 - s, 0)
            m = rows < 8 - s
        else:
            a_sh = pltpu.roll(a, s, 0)
            b_sh = pltpu.roll(b, s, 0)
            m = rows >= s
        b = jnp.where(m, a * b_sh + b, b)
        a = jnp.where(m, a * a_sh, a)
    return a, b


def _scan16(j, carry, a_scr, b_scr, rows, reverse):
    r0 = pl.multiple_of(j * 16, 16)
    a0, b0 = _scan8(a_scr[pl.ds(r0, 8), :], b_scr[pl.ds(r0, 8), :], rows, reverse)
    a1, b1 = _scan8(a_scr[pl.ds(r0 + 8, 8), :], b_scr[pl.ds(r0 + 8, 8), :], rows, reverse)
    if reverse:
        h1 = a1 * carry + b1
        h0 = a0 * h1[0:1, :] + b0
        carry = h0[0:1, :]
    else:
        h0 = a0 * carry + b0
        h1 = a1 * h0[7:8, :] + b1
        carry = h1[7:8, :]
    return r0, jnp.concatenate([h0, h1], axis=0), carry


def _lru_fwd_kernel(xprev_ref, xcur_ref, xnext_ref, cw_ref, cb_ref, gw_ref, gb_ref, lam_ref,
                    hf_ref, xpad_scr, a_scr, b_scr, carry_scr):
    c = pl.program_id(1)
    n_chunks = pl.num_programs(1)
    tl, d = xcur_ref.shape

    @pl.when(c == 0)
    def _():
        carry_scr[...] = jnp.zeros_like(carry_scr)

    _conv_and_gates(c, n_chunks, xprev_ref, xcur_ref, xnext_ref, cw_ref, cb_ref, gw_ref, gb_ref,
                    lam_ref, xpad_scr, a_scr, b_scr)
    rows = lax.broadcasted_iota(jnp.int32, (8, d), 0)

    def body(j, carry):
        r0, h, carry = _scan16(j, carry, a_scr, b_scr, rows, reverse=False)
        hf_ref[pl.ds(r0, 16), :] = h.astype(BF16)
        return carry

    carry_scr[...] = lax.fori_loop(0, tl // 16, body, carry_scr[...])


def _lru_bwd_kernel(xprev_ref, xcur_ref, xnext_ref, cw_ref, cb_ref, gw_ref, gb_ref, lam_ref,
                    hf_ref, y_ref, h_ref, wout_ref, g_ref,
                    out_ref, xpad_scr, a_scr, b_scr, carry_scr, z_scr):
    i = pl.program_id(1)
    n_chunks = pl.num_programs(1)
    c = n_chunks - 1 - i
    tl, d = xcur_ref.shape

    @pl.when(i == 0)
    def _():
        carry_scr[...] = jnp.zeros_like(carry_scr)

    _conv_and_gates(c, n_chunks, xprev_ref, xcur_ref, xnext_ref, cw_ref, cb_ref, gw_ref, gb_ref,
                    lam_ref, xpad_scr, a_scr, b_scr)
    rows = lax.broadcasted_iota(jnp.int32, (8, d), 0)
    n_tiles = tl // 16

    def body(jj, carry):
        r0, hb, carry = _scan16(n_tiles - 1 - jj, carry, a_scr, b_scr, rows, reverse=True)
        rec = hf_ref[pl.ds(r0, 16), :].astype(F32) + hb
        z_scr[pl.ds(r0, 16), :] = (rec * y_ref[pl.ds(r0, 16), :].astype(F32)).astype(BF16)
        return carry

    carry_scr[...] = lax.fori_loop(0, n_tiles, body, carry_scr[...])
    m = jnp.dot(z_scr[...], wout_ref[...], preferred_element_type=F32)
    out_ref[...] = h_ref[...] + _rms(m, g_ref[...])


def _scan_in_specs(L, D, chunk_of):
    tl = SCAN_TILE
    per = tl // HALO
    last = L // HALO - 1
    cur = pl.BlockSpec((None, tl, D), lambda b, i: (b, chunk_of(i), 0))
    prev = pl.BlockSpec((None, HALO, D), lambda b, i: (b, jnp.maximum(chunk_of(i) * per - 1, 0), 0))
    nxt = pl.BlockSpec((None, HALO, D), lambda b, i: (b, jnp.minimum((chunk_of(i) + 1) * per, last), 0))
    consts = [_const_spec((4, D)), _const_spec((1, D)),
              _const_spec((LRU_BLOCKS, LRU_BLOCK_W, 2 * LRU_BLOCK_W)), _const_spec((2, D)),
              _const_spec((1, D))]
    return cur, [prev, cur, nxt] + consts


def _scan_scratch(D):
    tl = SCAN_TILE
    return [pltpu.VMEM((tl + 16, D), F32), pltpu.VMEM((tl, D), F32), pltpu.VMEM((tl, D), F32),
            pltpu.VMEM((1, D), F32)]


def _lru_fwd(xb, cw, cb, gw, gb, lam):
    B, L, D = xb.shape
    cur, in_specs = _scan_in_specs(L, D, lambda i: i)
    return pl.pallas_call(
        _lru_fwd_kernel,
        out_shape=jax.ShapeDtypeStruct((B, L, D), BF16),
        grid=(B, L // SCAN_TILE),
        in_specs=in_specs,
        out_specs=cur,
        scratch_shapes=_scan_scratch(D),
        compiler_params=_params("parallel", "arbitrary"),
        name="lru_fwd",
    )(xb, xb, xb, cw, cb, gw, gb, lam)


def _lru_bwd(xb, cw, cb, gw, gb, lam, hf, y, h, w_out, g):
    B, L, D = xb.shape
    n_chunks = L // SCAN_TILE
    cur, in_specs = _scan_in_specs(L, D, lambda i: n_chunks - 1 - i)
    in_specs = in_specs + [cur, cur, cur, _const_spec((D, D)), _const_spec((1, D))]
    return pl.pallas_call(
        _lru_bwd_kernel,
        out_shape=jax.ShapeDtypeStruct((B, L, D), F32),
        grid=(B, n_chunks),
        in_specs=in_specs,
        out_specs=cur,
        scratch_shapes=_scan_scratch(D) + [pltpu.VMEM((SCAN_TILE, D), BF16)],
        compiler_params=_params("parallel", "arbitrary"),
        name="lru_bwd",
    )(xb, xb, xb, cw, cb, gw, gb, lam, hf, y, h, w_out, g)


def _rope(x, cos, sin_signed, first_half):
    partner = jnp.where(first_half, pltpu.roll(x, 96, 1), pltpu.roll(x, 32, 1))
    return x * cos + partner * sin_signed


def _qkv_kernel(h_ref, g_ref, w_ref, qg_ref, kg_ref, cos_ref, sin_ref, q_ref, k_ref, v_ref):
    u = _rms(h_ref[...], g_ref[...]).astype(BF16)
    hd = HEAD_DIM
    nq = N_HEADS * hd
    nk = N_KV_HEADS * hd
    cos = cos_ref[...]
    sin = sin_ref[...]
    first_half = (lax.broadcasted_iota(jnp.int32, cos.shape, 1) % 64) < 32
    q = jnp.dot(u, w_ref[:, :nq], preferred_element_type=F32)
    for hh in range(N_HEADS):
        sl = slice(hh * hd, (hh + 1) * hd)
        qh = _rope(_rms(q[:, sl], qg_ref[...]), cos, sin, first_half) * Q_SCALE
        q_ref[:, sl] = qh.astype(BF16)
    kv = jnp.dot(u, w_ref[:, nq:], preferred_element_type=F32)
    for hh in range(N_KV_HEADS):
        sl = slice(hh * hd, (hh + 1) * hd)
        k_ref[:, sl] = _rope(_rms(kv[:, sl], kg_ref[...]), cos, sin, first_half).astype(BF16)
    v_ref[...] = kv[:, nk:].astype(BF16)


def _qkv(h, g, w, qg, kg, cos, sin):
    B, L, D = h.shape
    nq = N_HEADS * HEAD_DIM
    nk = N_KV_HEADS * HEAD_DIM
    tab = pl.BlockSpec((TOKEN_TILE, HEAD_DIM), lambda b, i: (i, 0))
    return pl.pallas_call(
        _qkv_kernel,
        out_shape=(jax.ShapeDtypeStruct((B, L, nq), BF16), jax.ShapeDtypeStruct((B, L, nk), BF16),
                   jax.ShapeDtypeStruct((B, L, nk), BF16)),
        grid=(B, L // TOKEN_TILE),
        in_specs=[_tok_spec(TOKEN_TILE, D), _const_spec((1, D)), _const_spec((D, nq + 2 * nk)),
                  _const_spec((1, HEAD_DIM)), _const_spec((1, HEAD_DIM)), tab, tab],
        out_specs=(_tok_spec(TOKEN_TILE, nq), _tok_spec(TOKEN_TILE, nk), _tok_spec(TOKEN_TILE, nk)),
        compiler_params=_params("parallel", "parallel"),
        name="qkv",
    )(h, g, w, qg, kg, cos, sin)


def _attn_kernel(q_ref, k_ref, v_ref, o_ref, *, bounded):
    tq = q_ref.shape[0]
    L = k_ref.shape[0]
    hd = HEAD_DIM
    qs = [q_ref[:, g * hd:(g + 1) * hd] for g in range(GQA_GROUP)]
    ones_col = (lax.broadcasted_iota(jnp.int32, (KEY_CHUNK, hd), 1) == 0).astype(BF16)

    def scores(g, kc):
        return lax.dot_general(qs[g], kc, (((1,), (1,)), ((), ())), preferred_element_type=F32)

    def step(kc, vc, carry):
        if bounded:
            v_aug = jnp.concatenate([vc, ones_col[:vc.shape[0], :]], axis=-1)
            return tuple(acc + jnp.dot(jnp.exp2(scores(g, kc)).astype(BF16), v_aug, preferred_element_type=F32)
                         for g, acc in enumerate(carry))
        out = []
        for g in range(GQA_GROUP):
            m, l, acc = carry[g]
            s = scores(g, kc)
            m_new = jnp.maximum(m, jnp.max(s, axis=-1, keepdims=True))
            alpha = jnp.exp2(m - m_new)
            p = jnp.exp2(s - m_new)
            l = alpha * l + jnp.sum(p, axis=-1, keepdims=True)
            acc = alpha * acc + jnp.dot(p.astype(BF16), vc, preferred_element_type=F32)
            out.append((m_new, l, acc))
        return tuple(out)

    n_full = L // KEY_CHUNK
    tail = L - n_full * KEY_CHUNK
    if bounded:
        carry = tuple(jnp.zeros((tq, 2 * hd), F32) for _ in range(GQA_GROUP))
    else:
        carry = tuple((jnp.full((tq, 1), -jnp.inf, F32), jnp.zeros((tq, 1), F32), jnp.zeros((tq, hd), F32))
                      for _ in range(GQA_GROUP))
    if tail:
        carry = step(k_ref[pl.ds(n_full * KEY_CHUNK, tail), :], v_ref[pl.ds(n_full * KEY_CHUNK, tail), :],
                     carry)

    def body(j, carry):
        k0 = pl.multiple_of(j * KEY_CHUNK, KEY_CHUNK)
        return step(k_ref[pl.ds(k0, KEY_CHUNK), :], v_ref[pl.ds(k0, KEY_CHUNK), :], carry)

    carry = lax.fori_loop(0, n_full, body, carry, unroll=ATTN_UNROLL)
    for g in range(GQA_GROUP):
        if bounded:
            acc, l = carry[g][:, :hd], carry[g][:, hd:hd + 1]
        else:
            _, l, acc = carry[g]
        o_ref[:, g * hd:(g + 1) * hd] = (acc * (1.0 / l)).astype(BF16)


def _attention(q, k, v, *, bounded):
    B, L, nq = q.shape
    gw = GQA_GROUP * HEAD_DIM
    q_spec = pl.BlockSpec((None, Q_TILE, gw), lambda b, kh, i: (b, i, kh))
    kv_spec = pl.BlockSpec((None, L, HEAD_DIM), lambda b, kh, i: (b, 0, kh))
    return pl.pallas_call(
        functools.partial(_attn_kernel, bounded=bounded),
        out_shape=jax.ShapeDtypeStruct((B, L, nq), BF16),
        grid=(B, N_KV_HEADS, L // Q_TILE),
        in_specs=[q_spec, kv_spec, kv_spec],
        out_specs=q_spec,
        compiler_params=_params("parallel", "parallel", "parallel"),
        name="attention_bounded" if bounded else "attention_online",
    )(q, k, v)


def _outproj_kernel(z_ref, h_ref, w_ref, g_ref, out_ref):
    m = jnp.dot(z_ref[...], w_ref[...], preferred_element_type=F32)
    out_ref[...] = h_ref[...] + _rms(m, g_ref[...])


def _outproj(z, h, w, g):
    B, L, D = h.shape
    return pl.pallas_call(
        _outproj_kernel,
        out_shape=jax.ShapeDtypeStruct((B, L, D), F32),
        grid=(B, L // TOKEN_TILE),
        in_specs=[_tok_spec(TOKEN_TILE, z.shape[-1]), _tok_spec(TOKEN_TILE, D),
                  _const_spec(w.shape), _const_spec((1, D))],
        out_specs=_tok_spec(TOKEN_TILE, D),
        compiler_params=_params("parallel", "parallel"),
        name="attn_out",
    )(z, h, w, g)


def _ffn_kernel(h_ref, g_pre_ref, w_in_ref, w_out_ref, g_post_ref, out_ref):
    h = h_ref[...]
    u = _rms(h, g_pre_ref[...]).astype(BF16)
    d_ff = w_out_ref.shape[0]
    acc = None
    for c0 in range(0, d_ff, FF_CHUNK):
        gate = jnp.dot(u, w_in_ref[:, c0:c0 + FF_CHUNK], preferred_element_type=F32)
        up = jnp.dot(u, w_in_ref[:, d_ff + c0:d_ff + c0 + FF_CHUNK], preferred_element_type=F32)
        act = (gate * _sigmoid(gate) * up).astype(BF16)
        part = jnp.dot(act, w_out_ref[c0:c0 + FF_CHUNK, :], preferred_element_type=F32)
        acc = part if acc is None else acc + part
    out_ref[...] = h + _rms(acc, g_post_ref[...])


def _ffn(h, g_pre, w_in, w_out, g_post):
    B, L, D = h.shape
    return pl.pallas_call(
        _ffn_kernel,
        out_shape=jax.ShapeDtypeStruct((B, L, D), F32),
        grid=(B, L // TOKEN_TILE),
        in_specs=[_tok_spec(TOKEN_TILE, D), _const_spec((1, D)), _const_spec(w_in.shape),
                  _const_spec(w_out.shape), _const_spec((1, D))],
        out_specs=_tok_spec(TOKEN_TILE, D),
        compiler_params=_params("parallel", "parallel"),
        name="ffn",
    )(h, g_pre, w_in, w_out, g_post)


def _rope_tables(n_tokens):
    rows = n_tokens // GRID_W
    inv_freq = ROPE_THETA ** (-jnp.arange(0, 2 * ROPE_PAIRS, 2, dtype=F32) / (2 * ROPE_PAIRS))
    ang_r = jnp.arange(rows, dtype=F32)[:, None] * inv_freq
    ang_c = jnp.arange(GRID_W, dtype=F32)[:, None] * inv_freq
    ang_r = jnp.broadcast_to(ang_r[:, None, :], (rows, GRID_W, ROPE_PAIRS)).reshape(n_tokens, ROPE_PAIRS)
    ang_c = jnp.broadcast_to(ang_c[None, :, :], (rows, GRID_W, ROPE_PAIRS)).reshape(n_tokens, ROPE_PAIRS)
    ang = jnp.concatenate([ang_r, ang_r, ang_c, ang_c], axis=-1)
    ang = jnp.concatenate([jnp.zeros((N_META, HEAD_DIM), F32), ang], axis=0)
    sign = jnp.tile(jnp.concatenate([-jnp.ones((ROPE_PAIRS,), F32), jnp.ones((ROPE_PAIRS,), F32)]), 2)
    return jnp.cos(ang), jnp.sin(ang) * sign


def kernel(x, meta_tokens, norm_gains, lru_w_in, lru_conv_w, lru_conv_b, lru_gate_w, lru_gate_b,
           lru_lambda, lru_w_out, attn_w_qkv, attn_q_gain, attn_k_gain, attn_w_out, ffn_w_in,
           ffn_w_out):
    B, S, D = x.shape
    depth = norm_gains.shape[0]
    meta = jnp.broadcast_to(meta_tokens.astype(x.dtype)[None], (B, N_META, D))
    h = jnp.concatenate([meta, x], axis=1)
    cos, sin = _rope_tables(S)
    gains = norm_gains.reshape(depth, 4, 1, D)
    for layer in range(depth):
        g = gains[layer]
        slot = layer // 2
        if layer % 2 == 0:
            y, xb = _lru_in(h, g[0], lru_w_in[slot].astype(BF16))
            cw = lru_conv_w[slot]
            cb = lru_conv_b[slot].reshape(1, D)
            gw = jnp.concatenate([lru_gate_w[slot][:, 0], lru_gate_w[slot][:, 1]], axis=-1).astype(BF16)
            gb = lru_gate_b[slot]
            lam = lru_lambda[slot].reshape(2, 1, D)
            hf = _lru_fwd(xb, cw, cb, gw[0], gb[0], lam[0])
            h = _lru_bwd(xb, cw, cb, gw[1], gb[1], lam[1], hf, y, h, lru_w_out[slot].astype(BF16), g[1])
        else:
            q, k, v = _qkv(h, g[0], attn_w_qkv[slot].astype(BF16), attn_q_gain[slot].reshape(1, HEAD_DIM),
                           attn_k_gain[slot].reshape(1, HEAD_DIM), cos, sin)
            score_bound = (HEAD_DIM * Q_SCALE * jnp.max(jnp.abs(attn_q_gain[slot]))
                           * jnp.max(jnp.abs(attn_k_gain[slot])))
            o = lax.cond(score_bound <= SCORE_BOUND_MAX,
                         functools.partial(_attention, bounded=True),
                         functools.partial(_attention, bounded=False), q, k, v)
            h = _outproj(o, h, attn_w_out[slot].astype(BF16), g[1])
        h = _ffn(h, g[2], ffn_w_in[layer].astype(BF16), ffn_w_out[layer].astype(BF16), g[3])
    return h[:, N_META:]
```

```python
import functools

import jax
import jax.numpy as jnp
from jax import lax
from jax.experimental import pallas as pl
from jax.experimental.pallas import tpu as pltpu

F32 = jnp.float32
BF16 = jnp.bfloat16

D_MODEL = 1024
N_META = 16
GRID_W = 64
RMS_EPS = 1e-6
LRU_BLOCKS = 8
LRU_BLOCK_W = 128
LRU_C = 8.0
LOG2_E = 1.4426950408889634
F32_TINY = 1.1754943508222875e-38
HEAD_DIM = 128
N_HEADS = 8
N_KV_HEADS = 2
GQA_GROUP = 4
ROPE_PAIRS = 32
ROPE_THETA = 10000.0

VMEM_LIMIT_BYTES = 56 * 1024 * 1024

TOKEN_TILE = 912
SCAN_TILE = 432
Q_TILE = 912
KEY_CHUNK = 1024
ATTN_UNROLL = 2
MXU_DIM = 256
Q_SCALE = HEAD_DIM ** -0.5 * LOG2_E
SCORE_BOUND_MAX = 80.0
HALO = 16


def _params(*sem):
    return pltpu.CompilerParams(dimension_semantics=sem, vmem_limit_bytes=VMEM_LIMIT_BYTES)


def _tok_spec(tile, width):
    return pl.BlockSpec((None, tile, width), lambda b, i: (b, i, 0))


def _const_spec(shape):
    zeros = (0,) * len(shape)
    return pl.BlockSpec(shape, lambda b, i: zeros, pipeline_mode=pl.Buffered(1))


def _rms(x, g):
    return x * lax.rsqrt(jnp.mean(x * x, axis=-1, keepdims=True) + RMS_EPS) * g


def _sigmoid(x):
    return 0.5 * jnp.tanh(0.5 * x) + 0.5


def _gelu_tanh(x):
    return 0.5 * x * (1.0 + jnp.tanh(0.7978845608028654 * (x + 0.044715 * (x * x * x))))


def _lru_in_kernel(h_ref, g_ref, w_ref, y_ref, xb_ref):
    u = _rms(h_ref[...], g_ref[...]).astype(BF16)
    d = y_ref.shape[-1]
    y_ref[...] = _gelu_tanh(jnp.dot(u, w_ref[:, :d], preferred_element_type=F32)).astype(BF16)
    xb_ref[...] = jnp.dot(u, w_ref[:, d:], preferred_element_type=F32).astype(BF16)


def _lru_in(h, g, w):
    B, L, D = h.shape
    out = jax.ShapeDtypeStruct((B, L, D), BF16)
    return pl.pallas_call(
        _lru_in_kernel,
        out_shape=(out, out),
        grid=(B, L // TOKEN_TILE),
        in_specs=[_tok_spec(TOKEN_TILE, D), _const_spec((1, D)), _const_spec((D, 2 * D))],
        out_specs=(_tok_spec(TOKEN_TILE, D), _tok_spec(TOKEN_TILE, D)),
        compiler_params=_params("parallel", "parallel"),
        name="lru_in",
    )(h, g, w)


def _fill_xpad(c, n_chunks, xprev_ref, xcur_ref, xnext_ref, xpad_scr):
    tl = xcur_ref.shape[0]
    w = LRU_BLOCK_W
    prev = jnp.where(c > 0, xprev_ref[HALO - 8:, :].astype(F32), 0.0)
    nxt = jnp.where(c < n_chunks - 1, xnext_ref[:8, :].astype(F32), 0.0)
    for n in range(LRU_BLOCKS):
        sl = slice(n * w, (n + 1) * w)
        xpad_scr[n, pl.ds(0, 8), :] = prev[:, sl]
        xpad_scr[n, pl.ds(8, tl), :] = xcur_ref[:, sl].astype(F32)
        xpad_scr[n, pl.ds(8 + tl, 8), :] = nxt[:, sl]


def _half_log2_decay(lam_ref):
    x = -lam_ref[...]
    e = jnp.exp(-jnp.abs(x))
    u1 = 1.0 + e
    log1p_e = jnp.where(u1 == 1.0, e, jnp.log(u1) * (e / (u1 - 1.0)))
    return (-0.5 * LRU_C * LOG2_E) * (jnp.maximum(x, 0.0) + log1p_e)


def _scan_block(n, reverse, xpad_scr, cw_ref, cb_ref, gw_ref, half_decay, carry_scr, h_scr):
    w = LRU_BLOCK_W
    sl = slice(n * w, (n + 1) * w)
    seg = (xpad_scr.shape[1] - 16) // 8
    cb = 0.5 * cb_ref[:, sl]
    taps = [0.5 * cw_ref[k:k + 1, sl] for k in range(4)]
    tiles = []
    for r in range(seg):
        t = cb
        for k in range(4):
            t = t + xpad_scr[n, pl.ds(6 + r + k, 8, stride=seg), :] * taps[k]
        tiles.append(t)
    xh = jnp.concatenate(tiles, axis=0)
    ones = (lax.broadcasted_iota(jnp.int32, xh.shape, 1) < 2).astype(BF16)
    gp = jnp.dot(jnp.concatenate([xh.astype(BF16), ones], axis=-1), gw_ref[n],
                 preferred_element_type=F32)
    t_r = jnp.tanh(gp[:, :w])
    t_i = jnp.tanh(gp[:, w:])
    hd = half_decay[:, sl]
    log2_a = t_r * hd + hd
    a = jnp.exp2(log2_a)
    one_minus_a2 = jnp.tanh(log2_a * (-1.0 / LOG2_E)) * (1.0 + a * a)
    root = one_minus_a2 * lax.rsqrt(jnp.maximum(one_minus_a2, F32_TINY))
    b = root * (t_i * xh + xh)

    steps = range(seg - 1, -1, -1) if reverse else range(seg)
    hs = [None] * seg
    ps = [None] * seg
    h_prev = p_prev = None
    for r in steps:
        a_r = a[r * 8:(r + 1) * 8, :]
        b_r = b[r * 8:(r + 1) * 8, :]
        hs[r] = b_r if h_prev is None else a_r * h_prev + b_r
        ps[r] = a_r if p_prev is None else a_r * p_prev
        h_prev, p_prev = hs[r], ps[r]
    carry = carry_scr[n:n + 1, :]
    seg_in = [None] * 8
    for j in (range(7, -1, -1) if reverse else range(8)):
        seg_in[j] = carry
        carry = h_prev[j:j + 1, :] + p_prev[j:j + 1, :] * carry
    carry_scr[n:n + 1, :] = carry
    seg_in = jnp.concatenate(seg_in, axis=0)
    for r in range(seg):
        h_scr[n, pl.ds(r, 8, stride=seg), :] = hs[r] + ps[r] * seg_in


def _lru_fwd_kernel(xprev_ref, xcur_ref, xnext_ref, cw_ref, cb_ref, gw_ref, lam_ref,
                    hf_ref, xpad_scr, h_scr, carry_scr):
    c = pl.program_id(1)
    n_chunks = pl.num_programs(1)

    @pl.when(c == 0)
    def _():
        carry_scr[...] = jnp.zeros_like(carry_scr)

    _fill_xpad(c, n_chunks, xprev_ref, xcur_ref, xnext_ref, xpad_scr)
    half_decay = _half_log2_decay(lam_ref)
    w = LRU_BLOCK_W
    for n in range(LRU_BLOCKS):
        _scan_block(n, False, xpad_scr, cw_ref, cb_ref, gw_ref, half_decay, carry_scr, h_scr)
        hf_ref[:, n * w:(n + 1) * w] = h_scr[n].astype(BF16)


def _lru_bwd_kernel(xprev_ref, xcur_ref, xnext_ref, cw_ref, cb_ref, gw_ref, lam_ref,
                    hf_ref, y_ref, h_ref, wout_ref, g_ref,
                    out_ref, xpad_scr, h_scr, carry_scr, z_scr):
    i = pl.program_id(1)
    n_chunks = pl.num_programs(1)
    c = n_chunks - 1 - i

    @pl.when(i == 0)
    def _():
        carry_scr[...] = jnp.zeros_like(carry_scr)

    _fill_xpad(c, n_chunks, xprev_ref, xcur_ref, xnext_ref, xpad_scr)
    half_decay = _half_log2_decay(lam_ref)
    w = LRU_BLOCK_W
    for n in range(LRU_BLOCKS):
        sl = slice(n * w, (n + 1) * w)
        _scan_block(n, True, xpad_scr, cw_ref, cb_ref, gw_ref, half_decay, carry_scr, h_scr)
        rec = hf_ref[:, sl].astype(F32) + h_scr[n]
        z_scr[:, sl] = (rec * y_ref[:, sl].astype(F32)).astype(BF16)
    m = jnp.dot(z_scr[...], wout_ref[...], preferred_element_type=F32)
    out_ref[...] = h_ref[...] + _rms(m, g_ref[...])


def _scan_in_specs(L, D, chunk_of):
    tl = SCAN_TILE
    per = tl // HALO
    last = L // HALO - 1
    cur = pl.BlockSpec((None, tl, D), lambda b, i: (b, chunk_of(i), 0))
    prev = pl.BlockSpec((None, HALO, D), lambda b, i: (b, jnp.maximum(chunk_of(i) * per - 1, 0), 0))
    nxt = pl.BlockSpec((None, HALO, D), lambda b, i: (b, jnp.minimum((chunk_of(i) + 1) * per, last), 0))
    consts = [_const_spec((4, D)), _const_spec((1, D)),
              _const_spec((LRU_BLOCKS, 2 * LRU_BLOCK_W, 2 * LRU_BLOCK_W)), _const_spec((1, D))]
    return cur, [prev, cur, nxt] + consts


def _scan_scratch():
    tl = SCAN_TILE
    return [pltpu.VMEM((LRU_BLOCKS, tl + 16, LRU_BLOCK_W), F32), pltpu.VMEM((LRU_BLOCKS, tl, LRU_BLOCK_W), F32),
            pltpu.VMEM((LRU_BLOCKS, LRU_BLOCK_W), F32)]


def _lru_fwd(xb, cw, cb, gw, lam):
    B, L, D = xb.shape
    cur, in_specs = _scan_in_specs(L, D, lambda i: i)
    return pl.pallas_call(
        _lru_fwd_kernel,
        out_shape=jax.ShapeDtypeStruct((B, L, D), BF16),
        grid=(B, L // SCAN_TILE),
        in_specs=in_specs,
        out_specs=cur,
        scratch_shapes=_scan_scratch(),
        compiler_params=_params("parallel", "arbitrary"),
        name="lru_fwd",
    )(xb, xb, xb, cw, cb, gw, lam)


def _lru_bwd(xb, cw, cb, gw, lam, hf, y, h, w_out, g):
    B, L, D = xb.shape
    n_chunks = L // SCAN_TILE
    cur, in_specs = _scan_in_specs(L, D, lambda i: n_chunks - 1 - i)
    in_specs = in_specs + [cur, cur, cur, _const_spec((D, D)), _const_spec((1, D))]
    return pl.pallas_call(
        _lru_bwd_kernel,
        out_shape=jax.ShapeDtypeStruct((B, L, D), F32),
        grid=(B, n_chunks),
        in_specs=in_specs,
        out_specs=cur,
        scratch_shapes=_scan_scratch() + [pltpu.VMEM((SCAN_TILE, D), BF16)],
        compiler_params=_params("parallel", "arbitrary"),
        name="lru_bwd",
    )(xb, xb, xb, cw, cb, gw, lam, hf, y, h, w_out, g)


def _gate_weights(gate_w, gate_b):
    w = LRU_BLOCK_W
    wts = jnp.concatenate([gate_w[:, 0], gate_w[:, 1]], axis=-1)
    bias = 0.5 * gate_b.reshape(2, 2, LRU_BLOCKS, w).transpose(0, 2, 1, 3).reshape(2, LRU_BLOCKS, 1, 2 * w)
    hi = bias.astype(BF16)
    lo = (bias - hi.astype(F32)).astype(BF16)
    pad = jnp.zeros((2, LRU_BLOCKS, w - 2, 2 * w), BF16)
    return jnp.concatenate([wts.astype(BF16), hi, lo, pad], axis=2)


def _rope(x, cos, sin_signed, first_half):
    partner = jnp.where(first_half, pltpu.roll(x, 96, 1), pltpu.roll(x, 32, 1))
    return x * cos + partner * sin_signed


def _qkv_kernel(h_ref, g_ref, w_ref, qg_ref, kg_ref, cos_ref, sin_ref, q_ref, k_ref, v_ref):
    u = _rms(h_ref[...], g_ref[...]).astype(BF16)
    hd = HEAD_DIM
    nq = N_HEADS * hd
    nk = N_KV_HEADS * hd
    cos = cos_ref[...]
    sin = sin_ref[...]
    first_half = (lax.broadcasted_iota(jnp.int32, cos.shape, 1) % 64) < 32
    q = jnp.dot(u, w_ref[:, :nq], preferred_element_type=F32)
    for hh in range(N_HEADS):
        sl = slice(hh * hd, (hh + 1) * hd)
        qh = _rope(_rms(q[:, sl], qg_ref[...]), cos, sin, first_half) * Q_SCALE
        q_ref[:, sl] = qh.astype(BF16)
    kv = jnp.dot(u, w_ref[:, nq:], preferred_element_type=F32)
    for hh in range(N_KV_HEADS):
        sl = slice(hh * hd, (hh + 1) * hd)
        k_ref[:, sl] = _rope(_rms(kv[:, sl], kg_ref[...]), cos, sin, first_half).astype(BF16)
    v_ref[...] = kv[:, nk:].astype(BF16)


def _qkv(h, g, w, qg, kg, cos, sin):
    B, L, D = h.shape
    nq = N_HEADS * HEAD_DIM
    nk = N_KV_HEADS * HEAD_DIM
    tab = pl.BlockSpec((TOKEN_TILE, HEAD_DIM), lambda b, i: (i, 0))
    return pl.pallas_call(
        _qkv_kernel,
        out_shape=(jax.ShapeDtypeStruct((B, L, nq), BF16), jax.ShapeDtypeStruct((B, L, nk), BF16),
                   jax.ShapeDtypeStruct((B, L, nk), BF16)),
        grid=(B, L // TOKEN_TILE),
        in_specs=[_tok_spec(TOKEN_TILE, D), _const_spec((1, D)), _const_spec((D, nq + 2 * nk)),
                  _const_spec((1, HEAD_DIM)), _const_spec((1, HEAD_DIM)), tab, tab],
        out_specs=(_tok_spec(TOKEN_TILE, nq), _tok_spec(TOKEN_TILE, nk), _tok_spec(TOKEN_TILE, nk)),
        compiler_params=_params("parallel", "parallel"),
        name="qkv",
    )(h, g, w, qg, kg, cos, sin)


def _attn_kernel(q_ref, k_ref, v_ref, o_ref, *, bounded):
    tq = q_ref.shape[0]
    L = k_ref.shape[0]
    hd = HEAD_DIM
    qs = [q_ref[:, g * hd:(g + 1) * hd] for g in range(GQA_GROUP)]
    ones_col = (lax.broadcasted_iota(jnp.int32, (KEY_CHUNK, hd), 1) == 0).astype(BF16)

    def scores(g, kc):
        return lax.dot_general(qs[g], kc, (((1,), (1,)), ((), ())), preferred_element_type=F32)

    def step(kc, vc, carry):
        if bounded:
            v_aug = jnp.concatenate([vc, ones_col[:vc.shape[0], :]], axis=-1)
            return tuple(acc + jnp.dot(jnp.exp2(scores(g, kc)).astype(BF16), v_aug, preferred_element_type=F32)
                         for g, acc in enumerate(carry))
        out = []
        for g in range(GQA_GROUP):
            m, l, acc = carry[g]
            s = scores(g, kc)
            m_new = jnp.maximum(m, jnp.max(s, axis=-1, keepdims=True))
            alpha = jnp.exp2(m - m_new)
            p = jnp.exp2(s - m_new)
            l = alpha * l + jnp.sum(p, axis=-1, keepdims=True)
            acc = alpha * acc + jnp.dot(p.astype(BF16), vc, preferred_element_type=F32)
            out.append((m_new, l, acc))
        return tuple(out)

    n_full = L // KEY_CHUNK
    tail = L - n_full * KEY_CHUNK
    if bounded:
        carry = tuple(jnp.zeros((tq, 2 * hd), F32) for _ in range(GQA_GROUP))
    else:
        carry = tuple((jnp.full((tq, 1), -jnp.inf, F32), jnp.zeros((tq, 1), F32), jnp.zeros((tq, hd), F32))
                      for _ in range(GQA_GROUP))
    if tail:
        carry = step(k_ref[pl.ds(n_full * KEY_CHUNK, tail), :], v_ref[pl.ds(n_full * KEY_CHUNK, tail), :],
                     carry)

    def body(j, carry):
        k0 = pl.multiple_of(j * KEY_CHUNK, KEY_CHUNK)
        return step(k_ref[pl.ds(k0, KEY_CHUNK), :], v_ref[pl.ds(k0, KEY_CHUNK), :], carry)

    carry = lax.fori_loop(0, n_full, body, carry, unroll=ATTN_UNROLL)
    for g in range(GQA_GROUP):
        if bounded:
            acc, l = carry[g][:, :hd], carry[g][:, hd:hd + 1]
        else:
            _, l, acc = carry[g]
        o_ref[:, g * hd:(g + 1) * hd] = (acc * (1.0 / l)).astype(BF16)


def _attention(q, k, v, *, bounded):
    B, L, nq = q.shape
    gw = GQA_GROUP * HEAD_DIM
    q_spec = pl.BlockSpec((None, Q_TILE, gw), lambda b, kh, i: (b, i, kh))
    kv_spec = pl.BlockSpec((None, L, HEAD_DIM), lambda b, kh, i: (b, 0, kh))
    return pl.pallas_call(
        functools.partial(_attn_kernel, bounded=bounded),
        out_shape=jax.ShapeDtypeStruct((B, L, nq), BF16),
        grid=(B, N_KV_HEADS, L // Q_TILE),
        in_specs=[q_spec, kv_spec, kv_spec],
        out_specs=q_spec,
        compiler_params=_params("parallel", "parallel", "parallel"),
        name="attention_bounded" if bounded else "attention_online",
    )(q, k, v)


def _outproj_kernel(z_ref, h_ref, w_ref, g_ref, out_ref):
    m = jnp.dot(z_ref[...], w_ref[...], preferred_element_type=F32)
    out_ref[...] = h_ref[...] + _rms(m, g_ref[...])


def _outproj(z, h, w, g):
    B, L, D = h.shape
    return pl.pallas_call(
        _outproj_kernel,
        out_shape=jax.ShapeDtypeStruct((B, L, D), F32),
        grid=(B, L // TOKEN_TILE),
        in_specs=[_tok_spec(TOKEN_TILE, z.shape[-1]), _tok_spec(TOKEN_TILE, D),
                  _const_spec(w.shape), _const_spec((1, D))],
        out_specs=_tok_spec(TOKEN_TILE, D),
        compiler_params=_params("parallel", "parallel"),
        name="attn_out",
    )(z, h, w, g)


def _ffn_kernel(h_ref, g_pre_ref, w_in_ref, w_out_ref, g_post_ref, out_ref):
    h = h_ref[...]
    u = _rms(h, g_pre_ref[...]).astype(BF16)
    d_ff = w_out_ref.shape[0]
    acc = None
    split = (d_ff // MXU_DIM + 1) // 2 * MXU_DIM
    for c0, c1 in ((0, split), (split, d_ff)):
        gate = jnp.dot(u, w_in_ref[:, c0:c1], preferred_element_type=F32)
        up = jnp.dot(u, w_in_ref[:, d_ff + c0:d_ff + c1], preferred_element_type=F32)
        act = (gate * _sigmoid(gate) * up).astype(BF16)
        part = jnp.dot(act, w_out_ref[c0:c1, :], preferred_element_type=F32)
        acc = part if acc is None else acc + part
    out_ref[...] = h + _rms(acc, g_post_ref[...])


def _ffn(h, g_pre, w_in, w_out, g_post):
    B, L, D = h.shape
    return pl.pallas_call(
        _ffn_kernel,
        out_shape=jax.ShapeDtypeStruct((B, L, D), F32),
        grid=(B, L // TOKEN_TILE),
        in_specs=[_tok_spec(TOKEN_TILE, D), _const_spec((1, D)), _const_spec(w_in.shape),
                  _const_spec(w_out.shape), _const_spec((1, D))],
        out_specs=_tok_spec(TOKEN_TILE, D),
        compiler_params=_params("parallel", "parallel"),
        name="ffn",
    )(h, g_pre, w_in, w_out, g_post)


def _rope_tables(n_tokens):
    rows = n_tokens // GRID_W
    inv_freq = ROPE_THETA ** (-jnp.arange(0, 2 * ROPE_PAIRS, 2, dtype=F32) / (2 * ROPE_PAIRS))
    ang_r = jnp.arange(rows, dtype=F32)[:, None] * inv_freq
    ang_c = jnp.arange(GRID_W, dtype=F32)[:, None] * inv_freq
    ang_r = jnp.broadcast_to(ang_r[:, None, :], (rows, GRID_W, ROPE_PAIRS)).reshape(n_tokens, ROPE_PAIRS)
    ang_c = jnp.broadcast_to(ang_c[None, :, :], (rows, GRID_W, ROPE_PAIRS)).reshape(n_tokens, ROPE_PAIRS)
    ang = jnp.concatenate([ang_r, ang_r, ang_c, ang_c], axis=-1)
    ang = jnp.concatenate([jnp.zeros((N_META, HEAD_DIM), F32), ang], axis=0)
    sign = jnp.tile(jnp.concatenate([-jnp.ones((ROPE_PAIRS,), F32), jnp.ones((ROPE_PAIRS,), F32)]), 2)
    return jnp.cos(ang), jnp.sin(ang) * sign


def kernel(x, meta_tokens, norm_gains, lru_w_in, lru_conv_w, lru_conv_b, lru_gate_w, lru_gate_b,
           lru_lambda, lru_w_out, attn_w_qkv, attn_q_gain, attn_k_gain, attn_w_out, ffn_w_in,
           ffn_w_out):
    B, S, D = x.shape
    depth = norm_gains.shape[0]
    meta = jnp.broadcast_to(meta_tokens.astype(x.dtype)[None], (B, N_META, D))
    h = jnp.concatenate([meta, x], axis=1)
    cos, sin = _rope_tables(S)
    gains = norm_gains.reshape(depth, 4, 1, D)
    for layer in range(depth):
        g = gains[layer]
        slot = layer // 2
        if layer % 2 == 0:
            y, xb = _lru_in(h, g[0], lru_w_in[slot].astype(BF16))
            cw = lru_conv_w[slot]
            cb = lru_conv_b[slot].reshape(1, D)
            gw = _gate_weights(lru_gate_w[slot], lru_gate_b[slot])
            lam = lru_lambda[slot].reshape(2, 1, D)
            hf = _lru_fwd(xb, cw, cb, gw[0], lam[0])
            h = _lru_bwd(xb, cw, cb, gw[1], lam[1], hf, y, h, lru_w_out[slot].astype(BF16), g[1])
        else:
            q, k, v = _qkv(h, g[0], attn_w_qkv[slot].astype(BF16), attn_q_gain[slot].reshape(1, HEAD_DIM),
                           attn_k_gain[slot].reshape(1, HEAD_DIM), cos, sin)
            score_bound = (HEAD_DIM * Q_SCALE * jnp.max(jnp.abs(attn_q_gain[slot]))
                           * jnp.max(jnp.abs(attn_k_gain[slot])))
            o = lax.cond(score_bound <= SCORE_BOUND_MAX,
                         functools.partial(_attention, bounded=True),
                         functools.partial(_attention, bounded=False), q, k, v)
            h = _outproj(o, h, attn_w_out[slot].astype(BF16), g[1])
        h = _ffn(h, g[2], ffn_w_in[layer].astype(BF16), ffn_w_out[layer].astype(BF16), g[3])
    return h[:, N_META:]
```

```python
import functools

import jax
import jax.numpy as jnp
from jax import lax
from jax.experimental import pallas as pl
from jax.experimental.pallas import tpu as pltpu

F32 = jnp.float32
BF16 = jnp.bfloat16

D_MODEL = 1024
N_META = 16
GRID_W = 64
RMS_EPS = 1e-6
LRU_BLOCKS = 8
LRU_BLOCK_W = 128
LRU_C = 8.0
LOG2_E = 1.4426950408889634
F32_TINY = 1.1754943508222875e-38
HEAD_DIM = 128
N_HEADS = 8
N_KV_HEADS = 2
GQA_GROUP = 4
ROPE_PAIRS = 32
ROPE_THETA = 10000.0

VMEM_LIMIT_BYTES = 56 * 1024 * 1024

TOKEN_TILE = 912
SCAN_TILE = 432
Q_TILE = 912
KEY_CHUNK = 1024
ATTN_UNROLL = 2
FINAL_TILE = 1024
QKV_SUBTILES = 3
MXU_DIM = 256
Q_SCALE = HEAD_DIM ** -0.5 * LOG2_E
SCORE_BOUND_MAX = 80.0
BF16_ROWS = 16
HALO = BF16_ROWS


def _params(*sem):
    return pltpu.CompilerParams(dimension_semantics=sem, vmem_limit_bytes=VMEM_LIMIT_BYTES)


def _tok_spec(tile, width):
    return pl.BlockSpec((None, tile, width), lambda b, i: (b, i, 0))


def _const_spec(shape):
    zeros = (0,) * len(shape)
    return pl.BlockSpec(shape, lambda b, i: zeros, pipeline_mode=pl.Buffered(1))


def _rms(x, g):
    return x * lax.rsqrt(jnp.mean(x * x, axis=-1, keepdims=True) + RMS_EPS) * g


def _sigmoid(x):
    return 0.5 * jnp.tanh(0.5 * x) + 0.5


def _gelu_tanh(x):
    return 0.5 * x * (1.0 + jnp.tanh(0.7978845608028654 * (x + 0.044715 * (x * x * x))))


def _stream_specs(h, tile, chunk_of=lambda i: i):
    if isinstance(h, tuple):
        x, meta = h
        B, S, D = x.shape
        start = lambda b, i: pl.multiple_of(b * S + jnp.maximum(chunk_of(i) * tile - N_META, 0), 8)
        return ([x.reshape(B * S, D), meta],
                [pl.BlockSpec((pl.Element(tile), pl.Element(D)), lambda b, i: (start(b, i), 0)),
                 _const_spec(meta.shape)])
    return [h], [pl.BlockSpec((None, tile, h.shape[-1]), lambda b, i: (b, chunk_of(i), 0))]


def _stream_rows(refs, chunk):
    if len(refs) == 1:
        return refs[0][...]
    x_ref, meta_ref = refs
    xblk = x_ref[...]
    first = jnp.concatenate([meta_ref[...], xblk[:xblk.shape[0] - N_META, :]], axis=0)
    return jnp.where(chunk == 0, first, xblk)


def _lru_in_kernel(*refs):
    *h_refs, g_ref, w_ref, y_ref, xb_ref = refs
    u = _rms(_stream_rows(h_refs, pl.program_id(1)), g_ref[...]).astype(BF16)
    d = y_ref.shape[-1]
    y_ref[...] = _gelu_tanh(jnp.dot(u, w_ref[:, :d], preferred_element_type=F32)).astype(BF16)
    xb_ref[...] = jnp.dot(u, w_ref[:, d:], preferred_element_type=F32).astype(BF16)


def _lru_in(h, g, w, B, L):
    D = w.shape[0]
    out = jax.ShapeDtypeStruct((B, L, D), BF16)
    h_ops, h_specs = _stream_specs(h, TOKEN_TILE)
    return pl.pallas_call(
        _lru_in_kernel,
        out_shape=(out, out),
        grid=(B, L // TOKEN_TILE),
        in_specs=h_specs + [_const_spec((1, D)), _const_spec((D, 2 * D))],
        out_specs=(_tok_spec(TOKEN_TILE, D), _tok_spec(TOKEN_TILE, D)),
        compiler_params=_params("parallel", "parallel"),
        name="lru_in",
    )(*h_ops, g, w)


def _fill_xpad(c, n_chunks, xprev_ref, xcur_ref, xnext_ref, xpad_scr):
    tl = xcur_ref.shape[0]
    w = LRU_BLOCK_W
    prev = jnp.where(c > 0, xprev_ref[HALO - 8:, :].astype(F32), 0.0)
    nxt = jnp.where(c < n_chunks - 1, xnext_ref[:8, :].astype(F32), 0.0)
    for n in range(LRU_BLOCKS):
        sl = slice(n * w, (n + 1) * w)
        xpad_scr[n, pl.ds(0, 8), :] = prev[:, sl]
        xpad_scr[n, pl.ds(8, tl), :] = xcur_ref[:, sl].astype(F32)
        xpad_scr[n, pl.ds(8 + tl, 8), :] = nxt[:, sl]


def _half_log2_decay(lam_ref):
    x = -lam_ref[...]
    e = jnp.exp(-jnp.abs(x))
    u1 = 1.0 + e
    log1p_e = jnp.where(u1 == 1.0, e, jnp.log(u1) * (e / (u1 - 1.0)))
    return (-0.5 * LRU_C * LOG2_E) * (jnp.maximum(x, 0.0) + log1p_e)


def _scan_block(n, reverse, xpad_scr, cw_ref, cb_ref, gw_ref, half_decay, carry_scr, h_scr):
    w = LRU_BLOCK_W
    sl = slice(n * w, (n + 1) * w)
    seg = (xpad_scr.shape[1] - 16) // 8
    cb = 0.5 * cb_ref[:, sl]
    taps = [0.5 * cw_ref[k:k + 1, sl] for k in range(4)]
    tiles = []
    for r in range(seg):
        t = cb
        for k in range(4):
            t = t + xpad_scr[n, pl.ds(6 + r + k, 8, stride=seg), :] * taps[k]
        tiles.append(t)
    xh = jnp.concatenate(tiles, axis=0)
    ones = (lax.broadcasted_iota(jnp.int32, xh.shape, 1) < 2).astype(BF16)
    gp = jnp.dot(jnp.concatenate([xh.astype(BF16), ones], axis=-1), gw_ref[n],
                 preferred_element_type=F32)
    t_r = jnp.tanh(gp[:, :w])
    t_i = jnp.tanh(gp[:, w:])
    hd = half_decay[:, sl]
    log2_a = t_r * hd + hd
    a = jnp.exp2(log2_a)
    one_minus_a2 = jnp.tanh(log2_a * (-1.0 / LOG2_E)) * (1.0 + a * a)
    root = one_minus_a2 * lax.rsqrt(jnp.maximum(one_minus_a2, F32_TINY))
    b = root * (t_i * xh + xh)

    steps = range(seg - 1, -1, -1) if reverse else range(seg)
    hs = [None] * seg
    ps = [None] * seg
    h_prev = p_prev = None
    for r in steps:
        a_r = a[r * 8:(r + 1) * 8, :]
        b_r = b[r * 8:(r + 1) * 8, :]
        hs[r] = b_r if h_prev is None else a_r * h_prev + b_r
        ps[r] = a_r if p_prev is None else a_r * p_prev
        h_prev, p_prev = hs[r], ps[r]
    carry = carry_scr[n:n + 1, :]
    seg_in = [None] * 8
    for j in (range(7, -1, -1) if reverse else range(8)):
        seg_in[j] = carry
        carry = h_prev[j:j + 1, :] + p_prev[j:j + 1, :] * carry
    carry_scr[n:n + 1, :] = carry
    seg_in = jnp.concatenate(seg_in, axis=0)
    for r in range(seg):
        h_scr[n, pl.ds(r, 8, stride=seg), :] = hs[r] + ps[r] * seg_in


def _lru_fwd_kernel(xprev_ref, xcur_ref, xnext_ref, cw_ref, cb_ref, gw_ref, lam_ref,
                    hf_ref, xpad_scr, h_scr, carry_scr):
    c = pl.program_id(1)
    n_chunks = pl.num_programs(1)

    @pl.when(c == 0)
    def _():
        carry_scr[...] = jnp.zeros_like(carry_scr)

    _fill_xpad(c, n_chunks, xprev_ref, xcur_ref, xnext_ref, xpad_scr)
    half_decay = _half_log2_decay(lam_ref)
    w = LRU_BLOCK_W
    for n in range(LRU_BLOCKS):
        _scan_block(n, False, xpad_scr, cw_ref, cb_ref, gw_ref, half_decay, carry_scr, h_scr)
        hf_ref[:, n * w:(n + 1) * w] = h_scr[n].astype(BF16)


def _lru_bwd_kernel(xprev_ref, xcur_ref, xnext_ref, cw_ref, cb_ref, gw_ref, lam_ref,
                    hf_ref, y_ref, wout_ref, g_ref, *rest):
    *h_refs, out_ref, xpad_scr, h_scr, carry_scr, z_scr = rest
    i = pl.program_id(1)
    n_chunks = pl.num_programs(1)
    c = n_chunks - 1 - i

    @pl.when(i == 0)
    def _():
        carry_scr[...] = jnp.zeros_like(carry_scr)

    _fill_xpad(c, n_chunks, xprev_ref, xcur_ref, xnext_ref, xpad_scr)
    half_decay = _half_log2_decay(lam_ref)
    w = LRU_BLOCK_W
    for n in range(LRU_BLOCKS):
        sl = slice(n * w, (n + 1) * w)
        _scan_block(n, True, xpad_scr, cw_ref, cb_ref, gw_ref, half_decay, carry_scr, h_scr)
        rec = hf_ref[:, sl].astype(F32) + h_scr[n]
        z_scr[:, sl] = (rec * y_ref[:, sl].astype(F32)).astype(BF16)
    m = jnp.dot(z_scr[...], wout_ref[...], preferred_element_type=F32)
    out_ref[...] = _stream_rows(h_refs, c) + _rms(m, g_ref[...])


def _scan_in_specs(L, D, chunk_of):
    tl = SCAN_TILE
    per = tl // HALO
    last = L // HALO - 1
    cur = pl.BlockSpec((None, tl, D), lambda b, i: (b, chunk_of(i), 0))
    prev = pl.BlockSpec((None, HALO, D), lambda b, i: (b, jnp.maximum(chunk_of(i) * per - 1, 0), 0))
    nxt = pl.BlockSpec((None, HALO, D), lambda b, i: (b, jnp.minimum((chunk_of(i) + 1) * per, last), 0))
    consts = [_const_spec((4, D)), _const_spec((1, D)),
              _const_spec((LRU_BLOCKS, 2 * LRU_BLOCK_W, 2 * LRU_BLOCK_W)), _const_spec((1, D))]
    return cur, [prev, cur, nxt] + consts


def _scan_scratch():
    tl = SCAN_TILE
    return [pltpu.VMEM((LRU_BLOCKS, tl + 16, LRU_BLOCK_W), F32), pltpu.VMEM((LRU_BLOCKS, tl, LRU_BLOCK_W), F32),
            pltpu.VMEM((LRU_BLOCKS, LRU_BLOCK_W), F32)]


def _lru_fwd(xb, cw, cb, gw, lam):
    B, L, D = xb.shape
    cur, in_specs = _scan_in_specs(L, D, lambda i: i)
    return pl.pallas_call(
        _lru_fwd_kernel,
        out_shape=jax.ShapeDtypeStruct((B, L, D), BF16),
        grid=(B, L // SCAN_TILE),
        in_specs=in_specs,
        out_specs=cur,
        scratch_shapes=_scan_scratch(),
        compiler_params=_params("parallel", "arbitrary"),
        name="lru_fwd",
    )(xb, xb, xb, cw, cb, gw, lam)


def _lru_bwd(xb, cw, cb, gw, lam, hf, y, h, w_out, g):
    B, L, D = xb.shape
    n_chunks = L // SCAN_TILE
    cur, in_specs = _scan_in_specs(L, D, lambda i: n_chunks - 1 - i)
    h_ops, h_specs = _stream_specs(h, SCAN_TILE, lambda i: n_chunks - 1 - i)
    in_specs = in_specs + [cur, cur, _const_spec((D, D)), _const_spec((1, D))] + h_specs
    return pl.pallas_call(
        _lru_bwd_kernel,
        out_shape=jax.ShapeDtypeStruct((B, L, D), F32),
        grid=(B, n_chunks),
        in_specs=in_specs,
        out_specs=cur,
        scratch_shapes=_scan_scratch() + [pltpu.VMEM((SCAN_TILE, D), BF16)],
        compiler_params=_params("parallel", "arbitrary"),
        name="lru_bwd",
    )(xb, xb, xb, cw, cb, gw, lam, hf, y, w_out, g, *h_ops)


def _gate_weights(gate_w, gate_b):
    w = LRU_BLOCK_W
    wts = jnp.concatenate([gate_w[:, 0], gate_w[:, 1]], axis=-1)
    bias = 0.5 * gate_b.reshape(2, 2, LRU_BLOCKS, w).transpose(0, 2, 1, 3).reshape(2, LRU_BLOCKS, 1, 2 * w)
    hi = bias.astype(BF16)
    lo = (bias - hi.astype(F32)).astype(BF16)
    pad = jnp.zeros((2, LRU_BLOCKS, w - 2, 2 * w), BF16)
    return jnp.concatenate([wts.astype(BF16), hi, lo, pad], axis=2)


def _rope(x, cos, sin_signed):
    return x * cos + pltpu.roll(x, HEAD_DIM // 2, 1) * sin_signed


def _qkv_kernel(h_ref, g_ref, w_ref, qg_ref, kg_ref, cos_ref, sin_ref, q_ref, k_ref, v_ref):
    hd = HEAD_DIM
    nq = N_HEADS * hd
    nk = N_KV_HEADS * hd
    q_gain = qg_ref[...] * Q_SCALE
    sub = h_ref.shape[0] // QKV_SUBTILES
    for s in range(QKV_SUBTILES):
        rows = pl.ds(s * sub, sub)
        u = _rms(h_ref[rows, :], g_ref[...]).astype(BF16)
        cos = cos_ref[rows, :]
        sin = sin_ref[rows, :]
        q = jnp.dot(u, w_ref[:, :nq], preferred_element_type=F32)
        for hh in range(N_HEADS):
            sl = slice(hh * hd, (hh + 1) * hd)
            q_ref[rows, sl] = _rope(_rms(q[:, sl], q_gain), cos, sin).astype(BF16)
        kv = jnp.dot(u, w_ref[:, nq:], preferred_element_type=F32)
        for hh in range(N_KV_HEADS):
            sl = slice(hh * hd, (hh + 1) * hd)
            k_ref[rows, sl] = _rope(_rms(kv[:, sl], kg_ref[...]), cos, sin).astype(BF16)
        v_ref[rows, :] = kv[:, nk:].astype(BF16)


def _qkv(h, g, w, qg, kg, cos, sin):
    B, L, D = h.shape
    nq = N_HEADS * HEAD_DIM
    nk = N_KV_HEADS * HEAD_DIM
    tab = pl.BlockSpec((TOKEN_TILE, HEAD_DIM), lambda b, i: (i, 0))
    return pl.pallas_call(
        _qkv_kernel,
        out_shape=(jax.ShapeDtypeStruct((B, L, nq), BF16), jax.ShapeDtypeStruct((B, L, nk), BF16),
                   jax.ShapeDtypeStruct((B, L, nk), BF16)),
        grid=(B, L // TOKEN_TILE),
        in_specs=[_tok_spec(TOKEN_TILE, D), _const_spec((1, D)), _const_spec((D, nq + 2 * nk)),
                  _const_spec((1, HEAD_DIM)), _const_spec((1, HEAD_DIM)), tab, tab],
        out_specs=(_tok_spec(TOKEN_TILE, nq), _tok_spec(TOKEN_TILE, nk), _tok_spec(TOKEN_TILE, nk)),
        compiler_params=_params("parallel", "parallel"),
        name="qkv",
    )(h, g, w, qg, kg, cos, sin)


def _attn_kernel(q_ref, k_ref, v_ref, o_ref, *, bounded):
    tq = q_ref.shape[0]
    L = k_ref.shape[0]
    hd = HEAD_DIM
    qs = [q_ref[:, g * hd:(g + 1) * hd] for g in range(GQA_GROUP)]
    ones_col = (lax.broadcasted_iota(jnp.int32, (KEY_CHUNK, hd), 1) == 0).astype(BF16)

    def scores(g, kc):
        return lax.dot_general(qs[g], kc, (((1,), (1,)), ((), ())), preferred_element_type=F32)

    def step(kc, vc, carry):
        if bounded:
            v_aug = jnp.concatenate([vc, ones_col[:vc.shape[0], :]], axis=-1)
            return tuple(acc + jnp.dot(jnp.exp2(scores(g, kc)).astype(BF16), v_aug, preferred_element_type=F32)
                         for g, acc in enumerate(carry))
        out = []
        for g in range(GQA_GROUP):
            m, l, acc = carry[g]
            s = scores(g, kc)
            m_new = jnp.maximum(m, jnp.max(s, axis=-1, keepdims=True))
            alpha = jnp.exp2(m - m_new)
            p = jnp.exp2(s - m_new)
            l = alpha * l + jnp.sum(p, axis=-1, keepdims=True)
            acc = alpha * acc + jnp.dot(p.astype(BF16), vc, preferred_element_type=F32)
            out.append((m_new, l, acc))
        return tuple(out)

    n_full = L // KEY_CHUNK
    tail = L - n_full * KEY_CHUNK
    if bounded:
        carry = tuple(jnp.zeros((tq, 2 * hd), F32) for _ in range(GQA_GROUP))
    else:
        carry = tuple((jnp.full((tq, 1), -jnp.inf, F32), jnp.zeros((tq, 1), F32), jnp.zeros((tq, hd), F32))
                      for _ in range(GQA_GROUP))
    if tail:
        carry = step(k_ref[pl.ds(n_full * KEY_CHUNK, tail), :], v_ref[pl.ds(n_full * KEY_CHUNK, tail), :],
                     carry)

    def body(j, carry):
        k0 = pl.multiple_of(j * KEY_CHUNK, KEY_CHUNK)
        return step(k_ref[pl.ds(k0, KEY_CHUNK), :], v_ref[pl.ds(k0, KEY_CHUNK), :], carry)

    carry = lax.fori_loop(0, n_full, body, carry, unroll=ATTN_UNROLL)
    for g in range(GQA_GROUP):
        if bounded:
            acc, l = carry[g][:, :hd], carry[g][:, hd:hd + 1]
        else:
            _, l, acc = carry[g]
        o_ref[:, g * hd:(g + 1) * hd] = (acc * (1.0 / l)).astype(BF16)


def _attention(q, k, v, *, bounded):
    B, L, nq = q.shape
    gw = GQA_GROUP * HEAD_DIM
    q_spec = pl.BlockSpec((None, Q_TILE, gw), lambda b, kh, i: (b, i, kh))
    kv_spec = pl.BlockSpec((None, L, HEAD_DIM), lambda b, kh, i: (b, 0, kh))
    return pl.pallas_call(
        functools.partial(_attn_kernel, bounded=bounded),
        out_shape=jax.ShapeDtypeStruct((B, L, nq), BF16),
        grid=(B, N_KV_HEADS, L // Q_TILE),
        in_specs=[q_spec, kv_spec, kv_spec],
        out_specs=q_spec,
        compiler_params=_params("parallel", "parallel", "parallel"),
        name="attention_bounded" if bounded else "attention_online",
    )(q, k, v)


def _ffn_kernel(*refs, with_mixer_out):
    if with_mixer_out:
        z_ref, w_mix_ref, g_mix_ref, h_ref, g_pre_ref, w_in_ref, w_out_ref, g_post_ref, out_ref = refs
        m = jnp.dot(z_ref[...], w_mix_ref[...], preferred_element_type=F32)
        h = h_ref[...] + _rms(m, g_mix_ref[...])
    else:
        h_ref, g_pre_ref, w_in_ref, w_out_ref, g_post_ref, out_ref = refs
        h = h_ref[...]
    u = _rms(h, g_pre_ref[...]).astype(BF16)
    d_ff = w_out_ref.shape[0]
    acc = None
    split = (d_ff // MXU_DIM + 1) // 2 * MXU_DIM
    for c0, c1 in ((0, split), (split, d_ff)):
        gate = jnp.dot(u, w_in_ref[:, c0:c1], preferred_element_type=F32)
        up = jnp.dot(u, w_in_ref[:, d_ff + c0:d_ff + c1], preferred_element_type=F32)
        act = (gate * _sigmoid(gate) * up).astype(BF16)
        part = jnp.dot(act, w_out_ref[c0:c1, :], preferred_element_type=F32)
        acc = part if acc is None else acc + part
    out_ref[...] = h + _rms(acc, g_post_ref[...])


def _ffn(h, g_pre, w_in, w_out, g_post, mixer_out=None, drop_leading=0):
    B, L, D = h.shape
    n_out = L - drop_leading
    if drop_leading:
        tile = FINAL_TILE
        flat = lambda a: a.reshape(B * L, a.shape[-1])
        tok = lambda width: pl.BlockSpec((pl.Element(tile), pl.Element(width)),
                                         lambda b, i: (pl.multiple_of(b * L + drop_leading + i * tile, BF16_ROWS), 0))
        out_shape = jax.ShapeDtypeStruct((B * n_out, D), F32)
        out_spec = pl.BlockSpec((tile, D), lambda b, i: (b * (n_out // tile) + i, 0))
    else:
        tile = TOKEN_TILE
        flat = lambda a: a
        tok = lambda width: _tok_spec(tile, width)
        out_shape = jax.ShapeDtypeStruct((B, L, D), F32)
        out_spec = _tok_spec(tile, D)
    args = [flat(h), g_pre, w_in, w_out, g_post]
    in_specs = [tok(D), _const_spec((1, D)), _const_spec(w_in.shape), _const_spec(w_out.shape),
                _const_spec((1, D))]
    if mixer_out is not None:
        z, w_mix, g_mix = mixer_out
        args = [flat(z), w_mix, g_mix] + args
        in_specs = [tok(z.shape[-1]), _const_spec(w_mix.shape), _const_spec((1, D))] + in_specs
    out = pl.pallas_call(
        functools.partial(_ffn_kernel, with_mixer_out=mixer_out is not None),
        out_shape=out_shape,
        grid=(B, n_out // tile),
        in_specs=in_specs,
        out_specs=out_spec,
        compiler_params=_params("parallel", "parallel"),
        name="ffn_after_attention" if mixer_out is not None else "ffn",
    )(*args)
    return out.reshape(B, n_out, D)


def _rope_tables(n_tokens):
    rows = n_tokens // GRID_W
    inv_freq = ROPE_THETA ** (-jnp.arange(0, 2 * ROPE_PAIRS, 2, dtype=F32) / (2 * ROPE_PAIRS))
    ang_r = jnp.arange(rows, dtype=F32)[:, None] * inv_freq
    ang_c = jnp.arange(GRID_W, dtype=F32)[:, None] * inv_freq
    ang_r = jnp.broadcast_to(ang_r[:, None, :], (rows, GRID_W, ROPE_PAIRS)).reshape(n_tokens, ROPE_PAIRS)
    ang_c = jnp.broadcast_to(ang_c[None, :, :], (rows, GRID_W, ROPE_PAIRS)).reshape(n_tokens, ROPE_PAIRS)
    ang = jnp.concatenate([ang_r, ang_c, ang_r, ang_c], axis=-1)
    ang = jnp.concatenate([jnp.zeros((N_META, HEAD_DIM), F32), ang], axis=0)
    sign = jnp.concatenate([-jnp.ones((2 * ROPE_PAIRS,), F32), jnp.ones((2 * ROPE_PAIRS,), F32)])
    return jnp.cos(ang), jnp.sin(ang) * sign


def _head_dim_order():
    p = ROPE_PAIRS
    return jnp.concatenate([jnp.arange(0, p), jnp.arange(2 * p, 3 * p), jnp.arange(p, 2 * p),
                            jnp.arange(3 * p, 4 * p)])


def _reorder_qk_columns(w_qkv):
    order = _head_dim_order()
    n_qk = (N_HEADS + N_KV_HEADS) * HEAD_DIM
    cols = (jnp.arange(n_qk).reshape(-1, HEAD_DIM)[:, :1] + order[None, :]).reshape(-1)
    return jnp.concatenate([w_qkv[:, cols], w_qkv[:, n_qk:]], axis=1)


def kernel(x, meta_tokens, norm_gains, lru_w_in, lru_conv_w, lru_conv_b, lru_gate_w, lru_gate_b,
           lru_lambda, lru_w_out, attn_w_qkv, attn_q_gain, attn_k_gain, attn_w_out, ffn_w_in,
           ffn_w_out):
    B, S, D = x.shape
    depth = norm_gains.shape[0]
    L = N_META + S
    h = (x, meta_tokens.astype(x.dtype))
    cos, sin = _rope_tables(S)
    gains = norm_gains.reshape(depth, 4, 1, D)
    for layer in range(depth):
        g = gains[layer]
        slot = layer // 2
        mixer_out = None
        if layer % 2 == 0:
            y, xb = _lru_in(h, g[0], lru_w_in[slot].astype(BF16), B, L)
            cw = lru_conv_w[slot]
            cb = lru_conv_b[slot].reshape(1, D)
            gw = _gate_weights(lru_gate_w[slot], lru_gate_b[slot])
            lam = lru_lambda[slot].reshape(2, 1, D)
            hf = _lru_fwd(xb, cw, cb, gw[0], lam[0])
            h = _lru_bwd(xb, cw, cb, gw[1], lam[1], hf, y, h, lru_w_out[slot].astype(BF16), g[1])
        else:
            order = _head_dim_order()
            q, k, v = _qkv(h, g[0], _reorder_qk_columns(attn_w_qkv[slot]).astype(BF16),
                           attn_q_gain[slot][order].reshape(1, HEAD_DIM),
                           attn_k_gain[slot][order].reshape(1, HEAD_DIM), cos, sin)
            score_bound = (HEAD_DIM * Q_SCALE * jnp.max(jnp.abs(attn_q_gain[slot]))
                           * jnp.max(jnp.abs(attn_k_gain[slot])))
            o = lax.cond(score_bound <= SCORE_BOUND_MAX,
                         functools.partial(_attention, bounded=True),
                         functools.partial(_attention, bounded=False), q, k, v)
            mixer_out = (o, attn_w_out[slot].astype(BF16), g[1])
        h = _ffn(h, g[2], ffn_w_in[layer].astype(BF16), ffn_w_out[layer].astype(BF16), g[3], mixer_out,
                 drop_leading=N_META if layer == depth - 1 else 0)
    return h
```

```python
import functools

import jax
import jax.numpy as jnp
from jax import lax
from jax.experimental import pallas as pl
from jax.experimental.pallas import tpu as pltpu

F32 = jnp.float32
BF16 = jnp.bfloat16

D_MODEL = 1024
N_META = 16
GRID_W = 64
RMS_EPS = 1e-6
LRU_BLOCKS = 8
LRU_BLOCK_W = 128
LRU_C = 8.0
LOG2_E = 1.4426950408889634
F32_TINY = 1.1754943508222875e-38
HEAD_DIM = 128
N_HEADS = 8
N_KV_HEADS = 2
GQA_GROUP = 4
ROPE_PAIRS = 32
ROPE_THETA = 10000.0

VMEM_LIMIT_BYTES = 56 * 1024 * 1024

TOKEN_TILE = 912
SCAN_TILE = 432
FAST_ATTN_TILES = (2736, 512, 4)
ONLINE_ATTN_TILES = (912, 1024, 2)
FINAL_TILE = 1024
VT_ROWS = HEAD_DIM + 16
QKV_SUBTILES = 3
MXU_DIM = 256
Q_SCALE = HEAD_DIM ** -0.5 * LOG2_E
SCORE_BOUND_MAX = 80.0
BF16_ROWS = 16
HALO = BF16_ROWS


def _params(*sem):
    return pltpu.CompilerParams(dimension_semantics=sem, vmem_limit_bytes=VMEM_LIMIT_BYTES)


def _tok_spec(tile, width):
    return pl.BlockSpec((None, tile, width), lambda b, i: (b, i, 0))


def _const_spec(shape):
    zeros = (0,) * len(shape)
    return pl.BlockSpec(shape, lambda b, i: zeros, pipeline_mode=pl.Buffered(1))


def _rms(x, g):
    return x * lax.rsqrt(jnp.mean(x * x, axis=-1, keepdims=True) + RMS_EPS) * g


def _sigmoid(x):
    return 0.5 * jnp.tanh(0.5 * x) + 0.5


def _gelu_tanh(x):
    return 0.5 * x * (1.0 + jnp.tanh(0.7978845608028654 * (x + 0.044715 * (x * x * x))))


def _stream_specs(h, tile, chunk_of=lambda i: i):
    if isinstance(h, tuple):
        x, meta = h
        B, S, D = x.shape
        start = lambda b, i: pl.multiple_of(b * S + jnp.maximum(chunk_of(i) * tile - N_META, 0), 8)
        return ([x.reshape(B * S, D), meta],
                [pl.BlockSpec((pl.Element(tile), pl.Element(D)), lambda b, i: (start(b, i), 0)),
                 _const_spec(meta.shape)])
    return [h], [pl.BlockSpec((None, tile, h.shape[-1]), lambda b, i: (b, chunk_of(i), 0))]


def _stream_rows(refs, chunk):
    if len(refs) == 1:
        return refs[0][...]
    x_ref, meta_ref = refs
    xblk = x_ref[...]
    first = jnp.concatenate([meta_ref[...], xblk[:xblk.shape[0] - N_META, :]], axis=0)
    return jnp.where(chunk == 0, first, xblk)


def _lru_in_kernel(*refs):
    *h_refs, g_ref, w_ref, y_ref, xb_ref = refs
    u = _rms(_stream_rows(h_refs, pl.program_id(1)), g_ref[...]).astype(BF16)
    d = y_ref.shape[-1]
    y_ref[...] = _gelu_tanh(jnp.dot(u, w_ref[:, :d], preferred_element_type=F32)).astype(BF16)
    xb_ref[...] = jnp.dot(u, w_ref[:, d:], preferred_element_type=F32).astype(BF16)


def _lru_in(h, g, w, B, L):
    D = w.shape[0]
    out = jax.ShapeDtypeStruct((B, L, D), BF16)
    h_ops, h_specs = _stream_specs(h, TOKEN_TILE)
    return pl.pallas_call(
        _lru_in_kernel,
        out_shape=(out, out),
        grid=(B, L // TOKEN_TILE),
        in_specs=h_specs + [_const_spec((1, D)), _const_spec((D, 2 * D))],
        out_specs=(_tok_spec(TOKEN_TILE, D), _tok_spec(TOKEN_TILE, D)),
        compiler_params=_params("parallel", "parallel"),
        name="lru_in",
    )(*h_ops, g, w)


def _fill_xpad(c, n_chunks, xprev_ref, xcur_ref, xnext_ref, xpad_scr):
    tl = xcur_ref.shape[0]
    w = LRU_BLOCK_W
    prev = jnp.where(c > 0, xprev_ref[HALO - 8:, :].astype(F32), 0.0)
    nxt = jnp.where(c < n_chunks - 1, xnext_ref[:8, :].astype(F32), 0.0)
    for n in range(LRU_BLOCKS):
        sl = slice(n * w, (n + 1) * w)
        xpad_scr[n, pl.ds(0, 8), :] = prev[:, sl]
        xpad_scr[n, pl.ds(8, tl), :] = xcur_ref[:, sl].astype(F32)
        xpad_scr[n, pl.ds(8 + tl, 8), :] = nxt[:, sl]


def _half_log2_decay(lam_ref):
    x = -lam_ref[...]
    e = jnp.exp(-jnp.abs(x))
    u1 = 1.0 + e
    log1p_e = jnp.where(u1 == 1.0, e, jnp.log(u1) * (e / (u1 - 1.0)))
    return (-0.5 * LRU_C * LOG2_E) * (jnp.maximum(x, 0.0) + log1p_e)


def _scan_block(n, reverse, xpad_scr, cw_ref, cb_ref, gw_ref, half_decay, carry_scr, h_scr):
    w = LRU_BLOCK_W
    sl = slice(n * w, (n + 1) * w)
    seg = (xpad_scr.shape[1] - 16) // 8
    cb = 0.5 * cb_ref[:, sl]
    taps = [0.5 * cw_ref[k:k + 1, sl] for k in range(4)]
    tiles = []
    for r in range(seg):
        t = cb
        for k in range(4):
            t = t + xpad_scr[n, pl.ds(6 + r + k, 8, stride=seg), :] * taps[k]
        tiles.append(t)
    xh = jnp.concatenate(tiles, axis=0)
    ones = (lax.broadcasted_iota(jnp.int32, xh.shape, 1) < 2).astype(BF16)
    gp = jnp.dot(jnp.concatenate([xh.astype(BF16), ones], axis=-1), gw_ref[n],
                 preferred_element_type=F32)
    t_r = jnp.tanh(gp[:, :w])
    t_i = jnp.tanh(gp[:, w:])
    hd = half_decay[:, sl]
    log2_a = t_r * hd + hd
    a = jnp.exp2(log2_a)
    one_minus_a2 = jnp.tanh(log2_a * (-1.0 / LOG2_E)) * (1.0 + a * a)
    root = one_minus_a2 * lax.rsqrt(jnp.maximum(one_minus_a2, F32_TINY))
    b = root * (t_i * xh + xh)

    steps = range(seg - 1, -1, -1) if reverse else range(seg)
    hs = [None] * seg
    ps = [None] * seg
    h_prev = p_prev = None
    for r in steps:
        a_r = a[r * 8:(r + 1) * 8, :]
        b_r = b[r * 8:(r + 1) * 8, :]
        hs[r] = b_r if h_prev is None else a_r * h_prev + b_r
        ps[r] = a_r if p_prev is None else a_r * p_prev
        h_prev, p_prev = hs[r], ps[r]
    carry = carry_scr[n:n + 1, :]
    seg_in = [None] * 8
    for j in (range(7, -1, -1) if reverse else range(8)):
        seg_in[j] = carry
        carry = h_prev[j:j + 1, :] + p_prev[j:j + 1, :] * carry
    carry_scr[n:n + 1, :] = carry
    seg_in = jnp.concatenate(seg_in, axis=0)
    for r in range(seg):
        h_scr[n, pl.ds(r, 8, stride=seg), :] = hs[r] + ps[r] * seg_in


def _lru_fwd_kernel(xprev_ref, xcur_ref, xnext_ref, cw_ref, cb_ref, gw_ref, lam_ref,
                    hf_ref, xpad_scr, h_scr, carry_scr):
    c = pl.program_id(1)
    n_chunks = pl.num_programs(1)

    @pl.when(c == 0)
    def _():
        carry_scr[...] = jnp.zeros_like(carry_scr)

    _fill_xpad(c, n_chunks, xprev_ref, xcur_ref, xnext_ref, xpad_scr)
    half_decay = _half_log2_decay(lam_ref)
    w = LRU_BLOCK_W
    for n in range(LRU_BLOCKS):
        _scan_block(n, False, xpad_scr, cw_ref, cb_ref, gw_ref, half_decay, carry_scr, h_scr)
        hf_ref[:, n * w:(n + 1) * w] = h_scr[n].astype(BF16)


def _lru_bwd_kernel(xprev_ref, xcur_ref, xnext_ref, cw_ref, cb_ref, gw_ref, lam_ref,
                    hf_ref, y_ref, wout_ref, g_ref, *rest):
    *h_refs, out_ref, xpad_scr, h_scr, carry_scr, z_scr = rest
    i = pl.program_id(1)
    n_chunks = pl.num_programs(1)
    c = n_chunks - 1 - i

    @pl.when(i == 0)
    def _():
        carry_scr[...] = jnp.zeros_like(carry_scr)

    _fill_xpad(c, n_chunks, xprev_ref, xcur_ref, xnext_ref, xpad_scr)
    half_decay = _half_log2_decay(lam_ref)
    w = LRU_BLOCK_W
    for n in range(LRU_BLOCKS):
        sl = slice(n * w, (n + 1) * w)
        _scan_block(n, True, xpad_scr, cw_ref, cb_ref, gw_ref, half_decay, carry_scr, h_scr)
        rec = hf_ref[:, sl].astype(F32) + h_scr[n]
        z_scr[:, sl] = (rec * y_ref[:, sl].astype(F32)).astype(BF16)
    m = jnp.dot(z_scr[...], wout_ref[...], preferred_element_type=F32)
    out_ref[...] = _stream_rows(h_refs, c) + _rms(m, g_ref[...])


def _scan_in_specs(L, D, chunk_of):
    tl = SCAN_TILE
    per = tl // HALO
    last = L // HALO - 1
    cur = pl.BlockSpec((None, tl, D), lambda b, i: (b, chunk_of(i), 0))
    prev = pl.BlockSpec((None, HALO, D), lambda b, i: (b, jnp.maximum(chunk_of(i) * per - 1, 0), 0))
    nxt = pl.BlockSpec((None, HALO, D), lambda b, i: (b, jnp.minimum((chunk_of(i) + 1) * per, last), 0))
    consts = [_const_spec((4, D)), _const_spec((1, D)),
              _const_spec((LRU_BLOCKS, 2 * LRU_BLOCK_W, 2 * LRU_BLOCK_W)), _const_spec((1, D))]
    return cur, [prev, cur, nxt] + consts


def _scan_scratch():
    tl = SCAN_TILE
    return [pltpu.VMEM((LRU_BLOCKS, tl + 16, LRU_BLOCK_W), F32), pltpu.VMEM((LRU_BLOCKS, tl, LRU_BLOCK_W), F32),
            pltpu.VMEM((LRU_BLOCKS, LRU_BLOCK_W), F32)]


def _lru_fwd(xb, cw, cb, gw, lam):
    B, L, D = xb.shape
    cur, in_specs = _scan_in_specs(L, D, lambda i: i)
    return pl.pallas_call(
        _lru_fwd_kernel,
        out_shape=jax.ShapeDtypeStruct((B, L, D), BF16),
        grid=(B, L // SCAN_TILE),
        in_specs=in_specs,
        out_specs=cur,
        scratch_shapes=_scan_scratch(),
        compiler_params=_params("parallel", "arbitrary"),
        name="lru_fwd",
    )(xb, xb, xb, cw, cb, gw, lam)


def _lru_bwd(xb, cw, cb, gw, lam, hf, y, h, w_out, g):
    B, L, D = xb.shape
    n_chunks = L // SCAN_TILE
    cur, in_specs = _scan_in_specs(L, D, lambda i: n_chunks - 1 - i)
    h_ops, h_specs = _stream_specs(h, SCAN_TILE, lambda i: n_chunks - 1 - i)
    in_specs = in_specs + [cur, cur, _const_spec((D, D)), _const_spec((1, D))] + h_specs
    return pl.pallas_call(
        _lru_bwd_kernel,
        out_shape=jax.ShapeDtypeStruct((B, L, D), F32),
        grid=(B, n_chunks),
        in_specs=in_specs,
        out_specs=cur,
        scratch_shapes=_scan_scratch() + [pltpu.VMEM((SCAN_TILE, D), BF16)],
        compiler_params=_params("parallel", "arbitrary"),
        name="lru_bwd",
    )(xb, xb, xb, cw, cb, gw, lam, hf, y, w_out, g, *h_ops)


def _gate_weights(gate_w, gate_b):
    w = LRU_BLOCK_W
    wts = jnp.concatenate([gate_w[:, 0], gate_w[:, 1]], axis=-1)
    bias = 0.5 * gate_b.reshape(2, 2, LRU_BLOCKS, w).transpose(0, 2, 1, 3).reshape(2, LRU_BLOCKS, 1, 2 * w)
    hi = bias.astype(BF16)
    lo = (bias - hi.astype(F32)).astype(BF16)
    pad = jnp.zeros((2, LRU_BLOCKS, w - 2, 2 * w), BF16)
    return jnp.concatenate([wts.astype(BF16), hi, lo, pad], axis=2)


def _rope(x, cos, sin_signed):
    return x * cos + pltpu.roll(x, HEAD_DIM // 2, 1) * sin_signed


def _qkv_kernel(h_ref, g_ref, w_ref, qg_ref, kg_ref, cos_ref, sin_ref, q_ref, k_ref, v_ref):
    hd = HEAD_DIM
    nq = N_HEADS * hd
    nk = N_KV_HEADS * hd
    q_gain = qg_ref[...] * Q_SCALE
    sub = h_ref.shape[0] // QKV_SUBTILES
    for s in range(QKV_SUBTILES):
        rows = pl.ds(s * sub, sub)
        u = _rms(h_ref[rows, :], g_ref[...]).astype(BF16)
        cos = cos_ref[rows, :]
        sin = sin_ref[rows, :]
        q = jnp.dot(u, w_ref[:, :nq], preferred_element_type=F32)
        for hh in range(N_HEADS):
            sl = slice(hh * hd, (hh + 1) * hd)
            q_ref[rows, sl] = _rope(_rms(q[:, sl], q_gain), cos, sin).astype(BF16)
        kv = jnp.dot(u, w_ref[:, nq:], preferred_element_type=F32)
        for hh in range(N_KV_HEADS):
            sl = slice(hh * hd, (hh + 1) * hd)
            k_ref[rows, sl] = _rope(_rms(kv[:, sl], kg_ref[...]), cos, sin).astype(BF16)
        v_ref[rows, :] = kv[:, nk:].astype(BF16)


def _qkv(h, g, w, qg, kg, cos, sin):
    B, L, D = h.shape
    nq = N_HEADS * HEAD_DIM
    nk = N_KV_HEADS * HEAD_DIM
    tab = pl.BlockSpec((TOKEN_TILE, HEAD_DIM), lambda b, i: (i, 0))
    return pl.pallas_call(
        _qkv_kernel,
        out_shape=(jax.ShapeDtypeStruct((B, L, nq), BF16), jax.ShapeDtypeStruct((B, L, nk), BF16),
                   jax.ShapeDtypeStruct((B, L, nk), BF16)),
        grid=(B, L // TOKEN_TILE),
        in_specs=[_tok_spec(TOKEN_TILE, D), _const_spec((1, D)), _const_spec((D, nq + 2 * nk)),
                  _const_spec((1, HEAD_DIM)), _const_spec((1, HEAD_DIM)), tab, tab],
        out_specs=(_tok_spec(TOKEN_TILE, nq), _tok_spec(TOKEN_TILE, nk), _tok_spec(TOKEN_TILE, nk)),
        compiler_params=_params("parallel", "parallel"),
        name="qkv",
    )(h, g, w, qg, kg, cos, sin)


def _scores(q, kc):
    return lax.dot_general(q, kc, (((1,), (1,)), ((), ())), preferred_element_type=F32)


def _attn_bounded_kernel(q_ref, k_ref, vt_ref, o_ref, *, key_chunk, unroll):
    tq = q_ref.shape[0]
    L = k_ref.shape[0]
    hd = HEAD_DIM
    qs = [q_ref[:, g * hd:(g + 1) * hd] for g in range(GQA_GROUP)]

    def step(kc, vtc, accs):
        return tuple(acc + jnp.dot(vtc, jnp.exp2(_scores(kc, qs[g])).astype(BF16), preferred_element_type=F32)
                     for g, acc in enumerate(accs))

    n_full = L // key_chunk
    tail = L - n_full * key_chunk
    accs = tuple(jnp.zeros((vt_ref.shape[0], tq), F32) for _ in range(GQA_GROUP))
    if tail:
        accs = step(k_ref[pl.ds(n_full * key_chunk, tail), :], vt_ref[:, pl.ds(n_full * key_chunk, tail)], accs)

    def body(j, accs):
        k0 = pl.multiple_of(j * key_chunk, key_chunk)
        return step(k_ref[pl.ds(k0, key_chunk), :], vt_ref[:, pl.ds(k0, key_chunk)], accs)

    accs = lax.fori_loop(0, n_full, body, accs, unroll=unroll)
    for g in range(GQA_GROUP):
        out_t = accs[g][:hd, :] * (1.0 / accs[g][hd:hd + 1, :])
        o_ref[:, g * hd:(g + 1) * hd] = out_t.T.astype(BF16)


def _attn_online_kernel(q_ref, k_ref, v_ref, o_ref, *, key_chunk, unroll):
    tq = q_ref.shape[0]
    L = k_ref.shape[0]
    hd = HEAD_DIM
    qs = [q_ref[:, g * hd:(g + 1) * hd] for g in range(GQA_GROUP)]

    def step(kc, vc, carry):
        out = []
        for g in range(GQA_GROUP):
            m, l, acc = carry[g]
            s = _scores(qs[g], kc)
            m_new = jnp.maximum(m, jnp.max(s, axis=-1, keepdims=True))
            alpha = jnp.exp2(m - m_new)
            p = jnp.exp2(s - m_new)
            l = alpha * l + jnp.sum(p, axis=-1, keepdims=True)
            acc = alpha * acc + jnp.dot(p.astype(BF16), vc, preferred_element_type=F32)
            out.append((m_new, l, acc))
        return tuple(out)

    n_full = L // key_chunk
    tail = L - n_full * key_chunk
    carry = tuple((jnp.full((tq, 1), -jnp.inf, F32), jnp.zeros((tq, 1), F32), jnp.zeros((tq, hd), F32))
                  for _ in range(GQA_GROUP))
    if tail:
        carry = step(k_ref[pl.ds(n_full * key_chunk, tail), :], v_ref[pl.ds(n_full * key_chunk, tail), :],
                     carry)

    def body(j, carry):
        k0 = pl.multiple_of(j * key_chunk, key_chunk)
        return step(k_ref[pl.ds(k0, key_chunk), :], v_ref[pl.ds(k0, key_chunk), :], carry)

    carry = lax.fori_loop(0, n_full, body, carry, unroll=unroll)
    for g in range(GQA_GROUP):
        _, l, acc = carry[g]
        o_ref[:, g * hd:(g + 1) * hd] = (acc * (1.0 / l)).astype(BF16)


def _attention(q, k, v, *, bounded):
    B, L, nq = q.shape
    gw = GQA_GROUP * HEAD_DIM
    q_tile, key_chunk, unroll = FAST_ATTN_TILES if bounded else ONLINE_ATTN_TILES
    q_spec = pl.BlockSpec((None, q_tile, gw), lambda b, kh, i: (b, i, kh))
    kv_spec = pl.BlockSpec((None, L, HEAD_DIM), lambda b, kh, i: (b, 0, kh))
    if bounded:
        vt = v.reshape(B, L, N_KV_HEADS, HEAD_DIM).transpose(0, 2, 3, 1)
        extra = jnp.zeros((B, N_KV_HEADS, VT_ROWS - HEAD_DIM, L), BF16).at[:, :, 0, :].set(1.0)
        v_in = jnp.concatenate([vt, extra], axis=2)
        v_spec = pl.BlockSpec((None, None, VT_ROWS, L), lambda b, kh, i: (b, kh, 0, 0))
    else:
        v_in, v_spec = v, kv_spec
    return pl.pallas_call(
        functools.partial(_attn_bounded_kernel if bounded else _attn_online_kernel, key_chunk=key_chunk,
                          unroll=unroll),
        out_shape=jax.ShapeDtypeStruct((B, L, nq), BF16),
        grid=(B, N_KV_HEADS, L // q_tile),
        in_specs=[q_spec, kv_spec, v_spec],
        out_specs=q_spec,
        compiler_params=_params("parallel", "parallel", "parallel"),
        name="attention_bounded" if bounded else "attention_online",
    )(q, k, v_in)


def _ffn_kernel(*refs, with_mixer_out):
    if with_mixer_out:
        z_ref, w_mix_ref, g_mix_ref, h_ref, g_pre_ref, w_in_ref, w_out_ref, g_post_ref, out_ref = refs
        m = jnp.dot(z_ref[...], w_mix_ref[...], preferred_element_type=F32)
        h = h_ref[...] + _rms(m, g_mix_ref[...])
    else:
        h_ref, g_pre_ref, w_in_ref, w_out_ref, g_post_ref, out_ref = refs
        h = h_ref[...]
    u = _rms(h, g_pre_ref[...]).astype(BF16)
    d_ff = w_out_ref.shape[0]
    acc = None
    split = (d_ff // MXU_DIM + 1) // 2 * MXU_DIM
    for c0, c1 in ((0, split), (split, d_ff)):
        gate = jnp.dot(u, w_in_ref[:, c0:c1], preferred_element_type=F32)
        up = jnp.dot(u, w_in_ref[:, d_ff + c0:d_ff + c1], preferred_element_type=F32)
        act = (gate * _sigmoid(gate) * up).astype(BF16)
        part = jnp.dot(act, w_out_ref[c0:c1, :], preferred_element_type=F32)
        acc = part if acc is None else acc + part
    out_ref[...] = h + _rms(acc, g_post_ref[...])


def _ffn(h, g_pre, w_in, w_out, g_post, mixer_out=None, drop_leading=0):
    B, L, D = h.shape
    n_out = L - drop_leading
    if drop_leading:
        tile = FINAL_TILE
        flat = lambda a: a.reshape(B * L, a.shape[-1])
        tok = lambda width: pl.BlockSpec((pl.Element(tile), pl.Element(width)),
                                         lambda b, i: (pl.multiple_of(b * L + drop_leading + i * tile, BF16_ROWS), 0))
        out_shape = jax.ShapeDtypeStruct((B * n_out, D), F32)
        out_spec = pl.BlockSpec((tile, D), lambda b, i: (b * (n_out // tile) + i, 0))
    else:
        tile = TOKEN_TILE
        flat = lambda a: a
        tok = lambda width: _tok_spec(tile, width)
        out_shape = jax.ShapeDtypeStruct((B, L, D), F32)
        out_spec = _tok_spec(tile, D)
    args = [flat(h), g_pre, w_in, w_out, g_post]
    in_specs = [tok(D), _const_spec((1, D)), _const_spec(w_in.shape), _const_spec(w_out.shape),
                _const_spec((1, D))]
    if mixer_out is not None:
        z, w_mix, g_mix = mixer_out
        args = [flat(z), w_mix, g_mix] + args
        in_specs = [tok(z.shape[-1]), _const_spec(w_mix.shape), _const_spec((1, D))] + in_specs
    out = pl.pallas_call(
        functools.partial(_ffn_kernel, with_mixer_out=mixer_out is not None),
        out_shape=out_shape,
        grid=(B, n_out // tile),
        in_specs=in_specs,
        out_specs=out_spec,
        compiler_params=_params("parallel", "parallel"),
        name="ffn_after_attention" if mixer_out is not None else "ffn",
    )(*args)
    return out.reshape(B, n_out, D)


def _rope_tables(n_tokens):
    rows = n_tokens // GRID_W
    inv_freq = ROPE_THETA ** (-jnp.arange(0, 2 * ROPE_PAIRS, 2, dtype=F32) / (2 * ROPE_PAIRS))
    ang_r = jnp.arange(rows, dtype=F32)[:, None] * inv_freq
    ang_c = jnp.arange(GRID_W, dtype=F32)[:, None] * inv_freq
    ang_r = jnp.broadcast_to(ang_r[:, None, :], (rows, GRID_W, ROPE_PAIRS)).reshape(n_tokens, ROPE_PAIRS)
    ang_c = jnp.broadcast_to(ang_c[None, :, :], (rows, GRID_W, ROPE_PAIRS)).reshape(n_tokens, ROPE_PAIRS)
    ang = jnp.concatenate([ang_r, ang_c, ang_r, ang_c], axis=-1)
    ang = jnp.concatenate([jnp.zeros((N_META, HEAD_DIM), F32), ang], axis=0)
    sign = jnp.concatenate([-jnp.ones((2 * ROPE_PAIRS,), F32), jnp.ones((2 * ROPE_PAIRS,), F32)])
    return jnp.cos(ang), jnp.sin(ang) * sign


def _head_dim_order():
    p = ROPE_PAIRS
    return jnp.concatenate([jnp.arange(0, p), jnp.arange(2 * p, 3 * p), jnp.arange(p, 2 * p),
                            jnp.arange(3 * p, 4 * p)])


def _reorder_qk_columns(w_qkv):
    order = _head_dim_order()
    n_qk = (N_HEADS + N_KV_HEADS) * HEAD_DIM
    cols = (jnp.arange(n_qk).reshape(-1, HEAD_DIM)[:, :1] + order[None, :]).reshape(-1)
    return jnp.concatenate([w_qkv[:, cols], w_qkv[:, n_qk:]], axis=1)


def kernel(x, meta_tokens, norm_gains, lru_w_in, lru_conv_w, lru_conv_b, lru_gate_w, lru_gate_b,
           lru_lambda, lru_w_out, attn_w_qkv, attn_q_gain, attn_k_gain, attn_w_out, ffn_w_in,
           ffn_w_out):
    B, S, D = x.shape
    depth = norm_gains.shape[0]
    L = N_META + S
    h = (x, meta_tokens.astype(x.dtype))
    cos, sin = _rope_tables(S)
    gains = norm_gains.reshape(depth, 4, 1, D)
    for layer in range(depth):
        g = gains[layer]
        slot = layer // 2
        mixer_out = None
        if layer % 2 == 0:
            y, xb = _lru_in(h, g[0], lru_w_in[slot].astype(BF16), B, L)
            cw = lru_conv_w[slot]
            cb = lru_conv_b[slot].reshape(1, D)
            gw = _gate_weights(lru_gate_w[slot], lru_gate_b[slot])
            lam = lru_lambda[slot].reshape(2, 1, D)
            hf = _lru_fwd(xb, cw, cb, gw[0], lam[0])
            h = _lru_bwd(xb, cw, cb, gw[1], lam[1], hf, y, h, lru_w_out[slot].astype(BF16), g[1])
        else:
            order = _head_dim_order()
            q, k, v = _qkv(h, g[0], _reorder_qk_columns(attn_w_qkv[slot]).astype(BF16),
                           attn_q_gain[slot][order].reshape(1, HEAD_DIM),
                           attn_k_gain[slot][order].reshape(1, HEAD_DIM), cos, sin)
            score_bound = (HEAD_DIM * Q_SCALE * jnp.max(jnp.abs(attn_q_gain[slot]))
                           * jnp.max(jnp.abs(attn_k_gain[slot])))
            o = lax.cond(score_bound <= SCORE_BOUND_MAX,
                         functools.partial(_attention, bounded=True),
                         functools.partial(_attention, bounded=False), q, k, v)
            mixer_out = (o, attn_w_out[slot].astype(BF16), g[1])
        h = _ffn(h, g[2], ffn_w_in[layer].astype(BF16), ffn_w_out[layer].astype(BF16), g[3], mixer_out,
                 drop_leading=N_META if layer == depth - 1 else 0)
    return h
```

```python
import functools

import jax
import jax.numpy as jnp
from jax import lax
from jax.experimental import pallas as pl
from jax.experimental.pallas import tpu as pltpu

F32 = jnp.float32
BF16 = jnp.bfloat16

D_MODEL = 1024
N_META = 16
GRID_W = 64
RMS_EPS = 1e-6
LRU_BLOCKS = 8
LRU_BLOCK_W = 128
LRU_C = 8.0
LOG2_E = 1.4426950408889634
F32_TINY = 1.1754943508222875e-38
HEAD_DIM = 128
N_HEADS = 8
N_KV_HEADS = 2
GQA_GROUP = 4
ROPE_PAIRS = 32
ROPE_THETA = 10000.0

VMEM_LIMIT_BYTES = 56 * 1024 * 1024

TOKEN_TILE = 912
SCAN_TILE = 432
FAST_ATTN_TILES = (2736, 512, 4)
ONLINE_ATTN_TILES = (912, 1024, 2)
FINAL_TILE = 1024
QKV_SUBTILES = 3
MXU_DIM = 256
Q_SCALE = HEAD_DIM ** -0.5 * LOG2_E
SCORE_BOUND_MAX = 80.0
BF16_ROWS = 16
HALO = BF16_ROWS


def _params(*sem):
    return pltpu.CompilerParams(dimension_semantics=sem, vmem_limit_bytes=VMEM_LIMIT_BYTES)


def _tok_spec(tile, width):
    return pl.BlockSpec((None, tile, width), lambda b, i: (b, i, 0))


def _const_spec(shape):
    zeros = (0,) * len(shape)
    return pl.BlockSpec(shape, lambda b, i: zeros, pipeline_mode=pl.Buffered(1))


def _rms(x, g):
    return x * lax.rsqrt(jnp.mean(x * x, axis=-1, keepdims=True) + RMS_EPS) * g


def _sigmoid(x):
    return 0.5 * jnp.tanh(0.5 * x) + 0.5


def _gelu_tanh(x):
    return 0.5 * x * (1.0 + jnp.tanh(0.7978845608028654 * (x + 0.044715 * (x * x * x))))


def _stream_specs(h, tile, chunk_of=lambda i: i):
    if isinstance(h, tuple):
        x, meta = h
        B, S, D = x.shape
        start = lambda b, i: pl.multiple_of(b * S + jnp.maximum(chunk_of(i) * tile - N_META, 0), 8)
        return ([x.reshape(B * S, D), meta],
                [pl.BlockSpec((pl.Element(tile), pl.Element(D)), lambda b, i: (start(b, i), 0)),
                 _const_spec(meta.shape)])
    return [h], [pl.BlockSpec((None, tile, h.shape[-1]), lambda b, i: (b, chunk_of(i), 0))]


def _stream_rows(refs, chunk):
    if len(refs) == 1:
        return refs[0][...]
    x_ref, meta_ref = refs
    xblk = x_ref[...]
    first = jnp.concatenate([meta_ref[...], xblk[:xblk.shape[0] - N_META, :]], axis=0)
    return jnp.where(chunk == 0, first, xblk)


def _lru_in_kernel(*refs):
    *h_refs, g_ref, w_ref, y_ref, xb_ref = refs
    u = _rms(_stream_rows(h_refs, pl.program_id(1)), g_ref[...]).astype(BF16)
    d = y_ref.shape[-1]
    y_ref[...] = _gelu_tanh(jnp.dot(u, w_ref[:, :d], preferred_element_type=F32)).astype(BF16)
    xb_ref[...] = jnp.dot(u, w_ref[:, d:], preferred_element_type=F32).astype(BF16)


def _lru_in(h, g, w, B, L):
    D = w.shape[0]
    out = jax.ShapeDtypeStruct((B, L, D), BF16)
    h_ops, h_specs = _stream_specs(h, TOKEN_TILE)
    return pl.pallas_call(
        _lru_in_kernel,
        out_shape=(out, out),
        grid=(B, L // TOKEN_TILE),
        in_specs=h_specs + [_const_spec((1, D)), _const_spec((D, 2 * D))],
        out_specs=(_tok_spec(TOKEN_TILE, D), _tok_spec(TOKEN_TILE, D)),
        compiler_params=_params("parallel", "parallel"),
        name="lru_in",
    )(*h_ops, g, w)


def _fill_xpad(c, n_chunks, xprev_ref, xcur_ref, xnext_ref, xpad_scr):
    tl = xcur_ref.shape[0]
    w = LRU_BLOCK_W
    prev = jnp.where(c > 0, xprev_ref[HALO - 8:, :].astype(F32), 0.0)
    nxt = jnp.where(c < n_chunks - 1, xnext_ref[:8, :].astype(F32), 0.0)
    for n in range(LRU_BLOCKS):
        sl = slice(n * w, (n + 1) * w)
        xpad_scr[n, pl.ds(0, 8), :] = prev[:, sl]
        xpad_scr[n, pl.ds(8, tl), :] = xcur_ref[:, sl].astype(F32)
        xpad_scr[n, pl.ds(8 + tl, 8), :] = nxt[:, sl]


def _half_log2_decay(lam_ref):
    x = -lam_ref[...]
    e = jnp.exp(-jnp.abs(x))
    u1 = 1.0 + e
    log1p_e = jnp.where(u1 == 1.0, e, jnp.log(u1) * (e / (u1 - 1.0)))
    return (-0.5 * LRU_C * LOG2_E) * (jnp.maximum(x, 0.0) + log1p_e)


def _conv_block(n, xpad_scr, cw_ref, cb_ref):
    w = LRU_BLOCK_W
    sl = slice(n * w, (n + 1) * w)
    seg = (xpad_scr.shape[1] - 16) // 8
    cb = 0.5 * cb_ref[:, sl]
    taps = [0.5 * cw_ref[k:k + 1, sl] for k in range(4)]
    tiles = []
    for r in range(seg):
        t = cb
        for k in range(4):
            t = t + xpad_scr[n, pl.ds(6 + r + k, 8, stride=seg), :] * taps[k]
        tiles.append(t)
    return jnp.concatenate(tiles, axis=0)


def _gate_scan_block(n, reverse, xh, xh_bf16, gw_ref, half_decay, carry_scr, h_scr):
    w = LRU_BLOCK_W
    sl = slice(n * w, (n + 1) * w)
    seg = xh.shape[0] // 8
    ones = (lax.broadcasted_iota(jnp.int32, xh.shape, 1) < 2).astype(BF16)
    gp = jnp.dot(jnp.concatenate([xh_bf16, ones], axis=-1), gw_ref[n],
                 preferred_element_type=F32)
    t_r = jnp.tanh(gp[:, :w])
    t_i = jnp.tanh(gp[:, w:])
    hd = half_decay[:, sl]
    log2_a = t_r * hd + hd
    a = jnp.exp2(log2_a)
    one_minus_a2 = jnp.tanh(log2_a * (-1.0 / LOG2_E)) * (1.0 + a * a)
    root = one_minus_a2 * lax.rsqrt(jnp.maximum(one_minus_a2, F32_TINY))
    b = root * (t_i * xh + xh)

    steps = range(seg - 1, -1, -1) if reverse else range(seg)
    hs = [None] * seg
    ps = [None] * seg
    h_prev = p_prev = None
    for r in steps:
        a_r = a[r * 8:(r + 1) * 8, :]
        b_r = b[r * 8:(r + 1) * 8, :]
        hs[r] = b_r if h_prev is None else a_r * h_prev + b_r
        ps[r] = a_r if p_prev is None else a_r * p_prev
        h_prev, p_prev = hs[r], ps[r]
    carry = carry_scr[n:n + 1, :]
    seg_in = [None] * 8
    for j in (range(7, -1, -1) if reverse else range(8)):
        seg_in[j] = carry
        carry = h_prev[j:j + 1, :] + p_prev[j:j + 1, :] * carry
    carry_scr[n:n + 1, :] = carry
    seg_in = jnp.concatenate(seg_in, axis=0)
    for r in range(seg):
        h_scr[n, pl.ds(r, 8, stride=seg), :] = hs[r] + ps[r] * seg_in


def _lru_fwd_kernel(xprev_ref, xcur_ref, xnext_ref, cw_ref, cb_ref, gw_ref, lam_ref,
                    hf_ref, xs_ref, xpad_scr, h_scr, carry_scr):
    c = pl.program_id(1)
    n_chunks = pl.num_programs(1)

    @pl.when(c == 0)
    def _():
        carry_scr[...] = jnp.zeros_like(carry_scr)

    _fill_xpad(c, n_chunks, xprev_ref, xcur_ref, xnext_ref, xpad_scr)
    half_decay = _half_log2_decay(lam_ref)
    w = LRU_BLOCK_W
    for n in range(LRU_BLOCKS):
        sl = slice(n * w, (n + 1) * w)
        xh = _conv_block(n, xpad_scr, cw_ref, cb_ref)
        xh_bf16 = xh.astype(BF16)
        xs_ref[:, sl] = xh_bf16
        _gate_scan_block(n, False, xh, xh_bf16, gw_ref, half_decay, carry_scr, h_scr)
        hf_ref[:, sl] = h_scr[n].astype(BF16)


def _lru_bwd_kernel(xs_ref, gw_ref, lam_ref, hf_ref, y_ref, wout_ref, g_ref, *rest):
    *h_refs, out_ref, h_scr, carry_scr, z_scr = rest
    i = pl.program_id(1)
    n_chunks = pl.num_programs(1)
    c = n_chunks - 1 - i

    @pl.when(i == 0)
    def _():
        carry_scr[...] = jnp.zeros_like(carry_scr)

    half_decay = _half_log2_decay(lam_ref)
    w = LRU_BLOCK_W
    for n in range(LRU_BLOCKS):
        sl = slice(n * w, (n + 1) * w)
        xh_bf16 = xs_ref[:, sl]
        _gate_scan_block(n, True, xh_bf16.astype(F32), xh_bf16, gw_ref, half_decay, carry_scr, h_scr)
        rec = hf_ref[:, sl].astype(F32) + h_scr[n]
        z_scr[:, sl] = (rec * y_ref[:, sl].astype(F32)).astype(BF16)
    m = jnp.dot(z_scr[...], wout_ref[...], preferred_element_type=F32)
    out_ref[...] = _stream_rows(h_refs, c) + _rms(m, g_ref[...])


def _gate_spec():
    return _const_spec((LRU_BLOCKS, 2 * LRU_BLOCK_W, 2 * LRU_BLOCK_W))


def _lru_fwd(xb, cw, cb, gw, lam):
    B, L, D = xb.shape
    tl = SCAN_TILE
    per = tl // HALO
    last = L // HALO - 1
    cur = _tok_spec(tl, D)
    prev = pl.BlockSpec((None, HALO, D), lambda b, i: (b, jnp.maximum(i * per - 1, 0), 0))
    nxt = pl.BlockSpec((None, HALO, D), lambda b, i: (b, jnp.minimum((i + 1) * per, last), 0))
    out = jax.ShapeDtypeStruct((B, L, D), BF16)
    return pl.pallas_call(
        _lru_fwd_kernel,
        out_shape=(out, out),
        grid=(B, L // tl),
        in_specs=[prev, cur, nxt, _const_spec((4, D)), _const_spec((1, D)), _gate_spec(), _const_spec((1, D))],
        out_specs=(cur, cur),
        scratch_shapes=[pltpu.VMEM((LRU_BLOCKS, tl + 16, LRU_BLOCK_W), F32),
                        pltpu.VMEM((LRU_BLOCKS, tl, LRU_BLOCK_W), F32),
                        pltpu.VMEM((LRU_BLOCKS, LRU_BLOCK_W), F32)],
        compiler_params=_params("parallel", "arbitrary"),
        name="lru_fwd",
    )(xb, xb, xb, cw, cb, gw, lam)


def _lru_bwd(xs, gw, lam, hf, y, h, w_out, g):
    B, L, D = xs.shape
    tl = SCAN_TILE
    n_chunks = L // tl
    chunk_of = lambda i: n_chunks - 1 - i
    cur = pl.BlockSpec((None, tl, D), lambda b, i: (b, chunk_of(i), 0))
    h_ops, h_specs = _stream_specs(h, tl, chunk_of)
    return pl.pallas_call(
        _lru_bwd_kernel,
        out_shape=jax.ShapeDtypeStruct((B, L, D), F32),
        grid=(B, n_chunks),
        in_specs=[cur, _gate_spec(), _const_spec((1, D)), cur, cur, _const_spec((D, D)),
                  _const_spec((1, D))] + h_specs,
        out_specs=cur,
        scratch_shapes=[pltpu.VMEM((LRU_BLOCKS, tl, LRU_BLOCK_W), F32),
                        pltpu.VMEM((LRU_BLOCKS, LRU_BLOCK_W), F32),
                        pltpu.VMEM((tl, D), BF16)],
        compiler_params=_params("parallel", "arbitrary"),
        name="lru_bwd",
    )(xs, gw, lam, hf, y, w_out, g, *h_ops)


def _gate_weights(gate_w, gate_b):
    w = LRU_BLOCK_W
    wts = jnp.concatenate([gate_w[:, 0], gate_w[:, 1]], axis=-1)
    bias = 0.5 * gate_b.reshape(2, 2, LRU_BLOCKS, w).transpose(0, 2, 1, 3).reshape(2, LRU_BLOCKS, 1, 2 * w)
    hi = bias.astype(BF16)
    lo = (bias - hi.astype(F32)).astype(BF16)
    pad = jnp.zeros((2, LRU_BLOCKS, w - 2, 2 * w), BF16)
    return jnp.concatenate([wts.astype(BF16), hi, lo, pad], axis=2)


def _rope(x, cos, sin_signed):
    return x * cos + pltpu.roll(x, HEAD_DIM // 2, 1) * sin_signed


def _qkv_kernel(h_ref, g_ref, w_ref, qg_ref, kg_ref, cos_ref, sin_ref, q_ref, k_ref, v_ref):
    hd = HEAD_DIM
    nq = N_HEADS * hd
    nk = N_KV_HEADS * hd
    q_gain = qg_ref[...] * Q_SCALE
    sub = h_ref.shape[0] // QKV_SUBTILES
    for s in range(QKV_SUBTILES):
        rows = pl.ds(s * sub, sub)
        u = _rms(h_ref[rows, :], g_ref[...]).astype(BF16)
        cos = cos_ref[rows, :]
        sin = sin_ref[rows, :]
        q = jnp.dot(u, w_ref[:, :nq], preferred_element_type=F32)
        for hh in range(N_HEADS):
            sl = slice(hh * hd, (hh + 1) * hd)
            q_ref[rows, sl] = _rope(_rms(q[:, sl], q_gain), cos, sin).astype(BF16)
        kv = jnp.dot(u, w_ref[:, nq:], preferred_element_type=F32)
        for hh in range(N_KV_HEADS):
            sl = slice(hh * hd, (hh + 1) * hd)
            k_ref[rows, sl] = _rope(_rms(kv[:, sl], kg_ref[...]), cos, sin).astype(BF16)
        v_ref[rows, :] = kv[:, nk:].astype(BF16)


def _qkv(h, g, w, qg, kg, cos, sin):
    B, L, D = h.shape
    nq = N_HEADS * HEAD_DIM
    nk = N_KV_HEADS * HEAD_DIM
    tab = pl.BlockSpec((TOKEN_TILE, HEAD_DIM), lambda b, i: (i, 0))
    return pl.pallas_call(
        _qkv_kernel,
        out_shape=(jax.ShapeDtypeStruct((B, L, nq), BF16), jax.ShapeDtypeStruct((B, L, nk), BF16),
                   jax.ShapeDtypeStruct((B, L, nk), BF16)),
        grid=(B, L // TOKEN_TILE),
        in_specs=[_tok_spec(TOKEN_TILE, D), _const_spec((1, D)), _const_spec((D, nq + 2 * nk)),
                  _const_spec((1, HEAD_DIM)), _const_spec((1, HEAD_DIM)), tab, tab],
        out_specs=(_tok_spec(TOKEN_TILE, nq), _tok_spec(TOKEN_TILE, nk), _tok_spec(TOKEN_TILE, nk)),
        compiler_params=_params("parallel", "parallel"),
        name="qkv",
    )(h, g, w, qg, kg, cos, sin)


def _scores(q, kc):
    return lax.dot_general(q, kc, (((1,), (1,)), ((), ())), preferred_element_type=F32)


def _attn_bounded_kernel(q_ref, k_ref, vt_ref, o_ref, *, key_chunk, unroll):
    tq = q_ref.shape[0]
    L = k_ref.shape[0]
    hd = HEAD_DIM
    qs = [q_ref[:, g * hd:(g + 1) * hd] for g in range(GQA_GROUP)]

    def step(kc, vtc, carry):
        out = []
        for g, (acc, den) in enumerate(carry):
            p = jnp.exp2(_scores(kc, qs[g]))
            den = den + p.reshape(-1, 8, tq).sum(axis=0)
            out.append((acc + jnp.dot(vtc, p.astype(BF16), preferred_element_type=F32), den))
        return tuple(out)

    n_full = L // key_chunk
    tail = L - n_full * key_chunk
    carry = tuple((jnp.zeros((hd, tq), F32), jnp.zeros((8, tq), F32)) for _ in range(GQA_GROUP))
    if tail:
        carry = step(k_ref[pl.ds(n_full * key_chunk, tail), :], vt_ref[:, pl.ds(n_full * key_chunk, tail)], carry)

    def body(j, carry):
        k0 = pl.multiple_of(j * key_chunk, key_chunk)
        return step(k_ref[pl.ds(k0, key_chunk), :], vt_ref[:, pl.ds(k0, key_chunk)], carry)

    carry = lax.fori_loop(0, n_full, body, carry, unroll=unroll)
    for g in range(GQA_GROUP):
        acc, den = carry[g]
        out_t = acc * (1.0 / jnp.sum(den, axis=0, keepdims=True))
        o_ref[:, g * hd:(g + 1) * hd] = out_t.T.astype(BF16)


def _attn_online_kernel(q_ref, k_ref, v_ref, o_ref, *, key_chunk, unroll):
    tq = q_ref.shape[0]
    L = k_ref.shape[0]
    hd = HEAD_DIM
    qs = [q_ref[:, g * hd:(g + 1) * hd] for g in range(GQA_GROUP)]

    def step(kc, vc, carry):
        out = []
        for g in range(GQA_GROUP):
            m, l, acc = carry[g]
            s = _scores(qs[g], kc)
            m_new = jnp.maximum(m, jnp.max(s, axis=-1, keepdims=True))
            alpha = jnp.exp2(m - m_new)
            p = jnp.exp2(s - m_new)
            l = alpha * l + jnp.sum(p, axis=-1, keepdims=True)
            acc = alpha * acc + jnp.dot(p.astype(BF16), vc, preferred_element_type=F32)
            out.append((m_new, l, acc))
        return tuple(out)

    n_full = L // key_chunk
    tail = L - n_full * key_chunk
    carry = tuple((jnp.full((tq, 1), -jnp.inf, F32), jnp.zeros((tq, 1), F32), jnp.zeros((tq, hd), F32))
                  for _ in range(GQA_GROUP))
    if tail:
        carry = step(k_ref[pl.ds(n_full * key_chunk, tail), :], v_ref[pl.ds(n_full * key_chunk, tail), :],
                     carry)

    def body(j, carry):
        k0 = pl.multiple_of(j * key_chunk, key_chunk)
        return step(k_ref[pl.ds(k0, key_chunk), :], v_ref[pl.ds(k0, key_chunk), :], carry)

    carry = lax.fori_loop(0, n_full, body, carry, unroll=unroll)
    for g in range(GQA_GROUP):
        _, l, acc = carry[g]
        o_ref[:, g * hd:(g + 1) * hd] = (acc * (1.0 / l)).astype(BF16)


def _attention(q, k, v, *, bounded):
    B, L, nq = q.shape
    gw = GQA_GROUP * HEAD_DIM
    q_tile, key_chunk, unroll = FAST_ATTN_TILES if bounded else ONLINE_ATTN_TILES
    q_spec = pl.BlockSpec((None, q_tile, gw), lambda b, kh, i: (b, i, kh))
    kv_spec = pl.BlockSpec((None, L, HEAD_DIM), lambda b, kh, i: (b, 0, kh))
    if bounded:
        v_in = v.reshape(B, L, N_KV_HEADS, HEAD_DIM).transpose(0, 2, 3, 1)
        v_spec = pl.BlockSpec((None, None, HEAD_DIM, L), lambda b, kh, i: (b, kh, 0, 0))
    else:
        v_in, v_spec = v, kv_spec
    return pl.pallas_call(
        functools.partial(_attn_bounded_kernel if bounded else _attn_online_kernel, key_chunk=key_chunk,
                          unroll=unroll),
        out_shape=jax.ShapeDtypeStruct((B, L, nq), BF16),
        grid=(B, N_KV_HEADS, L // q_tile),
        in_specs=[q_spec, kv_spec, v_spec],
        out_specs=q_spec,
        compiler_params=_params("parallel", "parallel", "parallel"),
        name="attention_bounded" if bounded else "attention_online",
    )(q, k, v_in)


def _ffn_kernel(*refs, with_mixer_out):
    if with_mixer_out:
        z_ref, w_mix_ref, g_mix_ref, h_ref, g_pre_ref, w_in_ref, w_out_ref, g_post_ref, out_ref = refs
        m = jnp.dot(z_ref[...], w_mix_ref[...], preferred_element_type=F32)
        h = h_ref[...] + _rms(m, g_mix_ref[...])
    else:
        h_ref, g_pre_ref, w_in_ref, w_out_ref, g_post_ref, out_ref = refs
        h = h_ref[...]
    u = _rms(h, g_pre_ref[...]).astype(BF16)
    d_ff = w_out_ref.shape[0]
    acc = None
    split = (d_ff // MXU_DIM + 1) // 2 * MXU_DIM
    for c0, c1 in ((0, split), (split, d_ff)):
        gate = jnp.dot(u, w_in_ref[:, c0:c1], preferred_element_type=F32)
        up = jnp.dot(u, w_in_ref[:, d_ff + c0:d_ff + c1], preferred_element_type=F32)
        act = (gate * _sigmoid(gate) * up).astype(BF16)
        part = jnp.dot(act, w_out_ref[c0:c1, :], preferred_element_type=F32)
        acc = part if acc is None else acc + part
    out_ref[...] = h + _rms(acc, g_post_ref[...])


def _ffn(h, g_pre, w_in, w_out, g_post, mixer_out=None, drop_leading=0):
    B, L, D = h.shape
    n_out = L - drop_leading
    if drop_leading:
        tile = FINAL_TILE
        flat = lambda a: a.reshape(B * L, a.shape[-1])
        tok = lambda width: pl.BlockSpec((pl.Element(tile), pl.Element(width)),
                                         lambda b, i: (pl.multiple_of(b * L + drop_leading + i * tile, BF16_ROWS), 0))
        out_shape = jax.ShapeDtypeStruct((B * n_out, D), F32)
        out_spec = pl.BlockSpec((tile, D), lambda b, i: (b * (n_out // tile) + i, 0))
    else:
        tile = TOKEN_TILE
        flat = lambda a: a
        tok = lambda width: _tok_spec(tile, width)
        out_shape = jax.ShapeDtypeStruct((B, L, D), F32)
        out_spec = _tok_spec(tile, D)
    args = [flat(h), g_pre, w_in, w_out, g_post]
    in_specs = [tok(D), _const_spec((1, D)), _const_spec(w_in.shape), _const_spec(w_out.shape),
                _const_spec((1, D))]
    if mixer_out is not None:
        z, w_mix, g_mix = mixer_out
        args = [flat(z), w_mix, g_mix] + args
        in_specs = [tok(z.shape[-1]), _const_spec(w_mix.shape), _const_spec((1, D))] + in_specs
    out = pl.pallas_call(
        functools.partial(_ffn_kernel, with_mixer_out=mixer_out is not None),
        out_shape=out_shape,
        grid=(B, n_out // tile),
        in_specs=in_specs,
        out_specs=out_spec,
        compiler_params=_params("parallel", "parallel"),
        name="ffn_after_attention" if mixer_out is not None else "ffn",
    )(*args)
    return out.reshape(B, n_out, D)


def _rope_tables(n_tokens):
    rows = n_tokens // GRID_W
    inv_freq = ROPE_THETA ** (-jnp.arange(0, 2 * ROPE_PAIRS, 2, dtype=F32) / (2 * ROPE_PAIRS))
    ang_r = jnp.arange(rows, dtype=F32)[:, None] * inv_freq
    ang_c = jnp.arange(GRID_W, dtype=F32)[:, None] * inv_freq
    ang_r = jnp.broadcast_to(ang_r[:, None, :], (rows, GRID_W, ROPE_PAIRS)).reshape(n_tokens, ROPE_PAIRS)
    ang_c = jnp.broadcast_to(ang_c[None, :, :], (rows, GRID_W, ROPE_PAIRS)).reshape(n_tokens, ROPE_PAIRS)
    ang = jnp.concatenate([ang_r, ang_c, ang_r, ang_c], axis=-1)
    ang = jnp.concatenate([jnp.zeros((N_META, HEAD_DIM), F32), ang], axis=0)
    sign = jnp.concatenate([-jnp.ones((2 * ROPE_PAIRS,), F32), jnp.ones((2 * ROPE_PAIRS,), F32)])
    return jnp.cos(ang), jnp.sin(ang) * sign


def _head_dim_order():
    p = ROPE_PAIRS
    return jnp.concatenate([jnp.arange(0, p), jnp.arange(2 * p, 3 * p), jnp.arange(p, 2 * p),
                            jnp.arange(3 * p, 4 * p)])


def _reorder_qk_columns(w_qkv):
    order = _head_dim_order()
    n_qk = (N_HEADS + N_KV_HEADS) * HEAD_DIM
    cols = (jnp.arange(n_qk).reshape(-1, HEAD_DIM)[:, :1] + order[None, :]).reshape(-1)
    return jnp.concatenate([w_qkv[:, cols], w_qkv[:, n_qk:]], axis=1)


def kernel(x, meta_tokens, norm_gains, lru_w_in, lru_conv_w, lru_conv_b, lru_gate_w, lru_gate_b,
           lru_lambda, lru_w_out, attn_w_qkv, attn_q_gain, attn_k_gain, attn_w_out, ffn_w_in,
           ffn_w_out):
    B, S, D = x.shape
    depth = norm_gains.shape[0]
    L = N_META + S
    h = (x, meta_tokens.astype(x.dtype))
    cos, sin = _rope_tables(S)
    gains = norm_gains.reshape(depth, 4, 1, D)
    for layer in range(depth):
        g = gains[layer]
        slot = layer // 2
        mixer_out = None
        if layer % 2 == 0:
            y, xb = _lru_in(h, g[0], lru_w_in[slot].astype(BF16), B, L)
            cw = lru_conv_w[slot]
            cb = lru_conv_b[slot].reshape(1, D)
            gw = _gate_weights(lru_gate_w[slot], lru_gate_b[slot])
            lam = lru_lambda[slot].reshape(2, 1, D)
            hf, xs = _lru_fwd(xb, cw, cb, gw[0], lam[0])
            h = _lru_bwd(xs, gw[1], lam[1], hf, y, h, lru_w_out[slot].astype(BF16), g[1])
        else:
            order = _head_dim_order()
            q, k, v = _qkv(h, g[0], _reorder_qk_columns(attn_w_qkv[slot]).astype(BF16),
                           attn_q_gain[slot][order].reshape(1, HEAD_DIM),
                           attn_k_gain[slot][order].reshape(1, HEAD_DIM), cos, sin)
            score_bound = (HEAD_DIM * Q_SCALE * jnp.max(jnp.abs(attn_q_gain[slot]))
                           * jnp.max(jnp.abs(attn_k_gain[slot])))
            o = lax.cond(score_bound <= SCORE_BOUND_MAX,
                         functools.partial(_attention, bounded=True),
                         functools.partial(_attention, bounded=False), q, k, v)
            mixer_out = (o, attn_w_out[slot].astype(BF16), g[1])
        h = _ffn(h, g[2], ffn_w_in[layer].astype(BF16), ffn_w_out[layer].astype(BF16), g[3], mixer_out,
                 drop_leading=N_META if layer == depth - 1 else 0)
    return h
```

```python
import functools

import jax
import jax.numpy as jnp
from jax import lax
from jax.experimental import pallas as pl
from jax.experimental.pallas import tpu as pltpu

F32 = jnp.float32
BF16 = jnp.bfloat16

D_MODEL = 1024
N_META = 16
GRID_W = 64
RMS_EPS = 1e-6
LRU_BLOCKS = 8
LRU_BLOCK_W = 128
LRU_C = 8.0
LOG2_E = 1.4426950408889634
F32_TINY = 1.1754943508222875e-38
HEAD_DIM = 128
N_HEADS = 8
N_KV_HEADS = 2
GQA_GROUP = 4
ROPE_PAIRS = 32
ROPE_THETA = 10000.0

VMEM_LIMIT_BYTES = 56 * 1024 * 1024

TOKEN_TILE = 912
SCAN_TILE = 432
FAST_ATTN_TILES = (2736, 1024, 4)
ONLINE_ATTN_TILES = (912, 1024, 2)
FINAL_TILE = 1024
QKV_SUBTILES = 3
MXU_DIM = 256
Q_SCALE = HEAD_DIM ** -0.5 * LOG2_E
SCORE_BOUND_MAX = 80.0
BF16_ROWS = 16
HALO = BF16_ROWS


def _params(*sem):
    return pltpu.CompilerParams(dimension_semantics=sem, vmem_limit_bytes=VMEM_LIMIT_BYTES)


def _tok_spec(tile, width):
    return pl.BlockSpec((None, tile, width), lambda b, i: (b, i, 0))


def _const_spec(shape):
    zeros = (0,) * len(shape)
    return pl.BlockSpec(shape, lambda b, i: zeros, pipeline_mode=pl.Buffered(1))


def _rms(x, g):
    return x * lax.rsqrt(jnp.mean(x * x, axis=-1, keepdims=True) + RMS_EPS) * g


def _sigmoid(x):
    return 0.5 * jnp.tanh(0.5 * x) + 0.5


def _gelu_tanh(x):
    return 0.5 * x * (1.0 + jnp.tanh(0.7978845608028654 * (x + 0.044715 * (x * x * x))))


def _stream_specs(h, tile, chunk_of=lambda i: i):
    if isinstance(h, tuple):
        x, meta = h
        B, S, D = x.shape
        start = lambda b, i: pl.multiple_of(b * S + jnp.maximum(chunk_of(i) * tile - N_META, 0), 8)
        return ([x.reshape(B * S, D), meta],
                [pl.BlockSpec((pl.Element(tile), pl.Element(D)), lambda b, i: (start(b, i), 0)),
                 _const_spec(meta.shape)])
    return [h], [pl.BlockSpec((None, tile, h.shape[-1]), lambda b, i: (b, chunk_of(i), 0))]


def _stream_rows(refs, chunk):
    if len(refs) == 1:
        return refs[0][...]
    x_ref, meta_ref = refs
    xblk = x_ref[...]
    first = jnp.concatenate([meta_ref[...], xblk[:xblk.shape[0] - N_META, :]], axis=0)
    return jnp.where(chunk == 0, first, xblk)


def _lru_in_kernel(*refs):
    *h_refs, g_ref, w_ref, y_ref, xb_ref = refs
    u = _rms(_stream_rows(h_refs, pl.program_id(1)), g_ref[...]).astype(BF16)
    d = y_ref.shape[-1]
    y_ref[...] = _gelu_tanh(jnp.dot(u, w_ref[:, :d], preferred_element_type=F32)).astype(BF16)
    xb_ref[...] = jnp.dot(u, w_ref[:, d:], preferred_element_type=F32).astype(BF16)


def _lru_in(h, g, w, B, L):
    D = w.shape[0]
    out = jax.ShapeDtypeStruct((B, L, D), BF16)
    h_ops, h_specs = _stream_specs(h, TOKEN_TILE)
    return pl.pallas_call(
        _lru_in_kernel,
        out_shape=(out, out),
        grid=(B, L // TOKEN_TILE),
        in_specs=h_specs + [_const_spec((1, D)), _const_spec((D, 2 * D))],
        out_specs=(_tok_spec(TOKEN_TILE, D), _tok_spec(TOKEN_TILE, D)),
        compiler_params=_params("parallel", "parallel"),
        name="lru_in",
    )(*h_ops, g, w)


def _fill_xpad(c, n_chunks, xprev_ref, xcur_ref, xnext_ref, xpad_scr):
    tl = xcur_ref.shape[0]
    w = LRU_BLOCK_W
    prev = jnp.where(c > 0, xprev_ref[HALO - 8:, :].astype(F32), 0.0)
    nxt = jnp.where(c < n_chunks - 1, xnext_ref[:8, :].astype(F32), 0.0)
    for n in range(LRU_BLOCKS):
        sl = slice(n * w, (n + 1) * w)
        xpad_scr[n, pl.ds(0, 8), :] = prev[:, sl]
        xpad_scr[n, pl.ds(8, tl), :] = xcur_ref[:, sl].astype(F32)
        xpad_scr[n, pl.ds(8 + tl, 8), :] = nxt[:, sl]


def _half_log2_decay(lam_ref):
    x = -lam_ref[...]
    e = jnp.exp(-jnp.abs(x))
    u1 = 1.0 + e
    log1p_e = jnp.where(u1 == 1.0, e, jnp.log(u1) * (e / (u1 - 1.0)))
    return (-0.5 * LRU_C * LOG2_E) * (jnp.maximum(x, 0.0) + log1p_e)


def _conv_block(n, xpad_blk, cw_ref, cb_ref):
    w = LRU_BLOCK_W
    sl = slice(n * w, (n + 1) * w)
    seg = (xpad_blk.shape[0] - 16) // 8
    cb = 0.5 * cb_ref[:, sl]
    taps = [0.5 * cw_ref[k:k + 1, sl] for k in range(4)]
    tiles = []
    for r in range(seg):
        t = cb
        for k in range(4):
            t = t + xpad_blk[pl.ds(6 + r + k, 8, stride=seg), :] * taps[k]
        tiles.append(t)
    return jnp.concatenate(tiles, axis=0)


def _gate_scan_block(n, reverse, xh, xh_bf16, gw_ref, half_decay, carry_blk, h_blk):
    w = LRU_BLOCK_W
    sl = slice(n * w, (n + 1) * w)
    seg = xh.shape[0] // 8
    ones = (lax.broadcasted_iota(jnp.int32, xh.shape, 1) < 2).astype(BF16)
    gp = jnp.dot(jnp.concatenate([xh_bf16, ones], axis=-1), gw_ref[n],
                 preferred_element_type=F32)
    t_r = jnp.tanh(gp[:, :w])
    t_i = jnp.tanh(gp[:, w:])
    hd = half_decay[:, sl]
    log2_a = t_r * hd + hd
    a = jnp.exp2(log2_a)
    one_minus_a2 = jnp.tanh(log2_a * (-1.0 / LOG2_E)) * (1.0 + a * a)
    root = one_minus_a2 * lax.rsqrt(jnp.maximum(one_minus_a2, F32_TINY))
    b = root * (t_i * xh + xh)

    steps = range(seg - 1, -1, -1) if reverse else range(seg)
    hs = [None] * seg
    ps = [None] * seg
    h_prev = p_prev = None
    for r in steps:
        a_r = a[r * 8:(r + 1) * 8, :]
        b_r = b[r * 8:(r + 1) * 8, :]
        hs[r] = b_r if h_prev is None else a_r * h_prev + b_r
        ps[r] = a_r if p_prev is None else a_r * p_prev
        h_prev, p_prev = hs[r], ps[r]
    carry = carry_blk[...]
    seg_in = [None] * 8
    for j in (range(7, -1, -1) if reverse else range(8)):
        seg_in[j] = carry
        carry = h_prev[j:j + 1, :] + p_prev[j:j + 1, :] * carry
    carry_blk[...] = carry
    seg_in = jnp.concatenate(seg_in, axis=0)
    for r in range(seg):
        h_blk[pl.ds(r, 8, stride=seg), :] = hs[r] + ps[r] * seg_in


def _lru_fwd_kernel(xprev_ref, xcur_ref, xnext_ref, cw_ref, cb_ref, gw_ref, lam_ref,
                    hf_ref, xs_ref, xpad_scr, h_scr, carry_scr):
    c = pl.program_id(1)
    n_chunks = pl.num_programs(1)

    @pl.when(c == 0)
    def _():
        carry_scr[...] = jnp.zeros_like(carry_scr)

    _fill_xpad(c, n_chunks, xprev_ref, xcur_ref, xnext_ref, xpad_scr)
    half_decay = _half_log2_decay(lam_ref)
    w = LRU_BLOCK_W
    for n in range(LRU_BLOCKS):
        sl = slice(n * w, (n + 1) * w)
        xh = _conv_block(n, xpad_scr.at[n], cw_ref, cb_ref)
        xh_bf16 = xh.astype(BF16)
        xs_ref[:, sl] = xh_bf16
        _gate_scan_block(n, False, xh, xh_bf16, gw_ref, half_decay, carry_scr.at[pl.ds(n, 1)], h_scr.at[n])
        hf_ref[:, sl] = h_scr[n].astype(BF16)


def _lru_bwd_kernel(xs_ref, gw_ref, lam_ref, hf_ref, y_ref, wout_ref, g_ref, *rest):
    *h_refs, out_ref, h_scr, carry_scr, z_scr = rest
    i = pl.program_id(1)
    n_chunks = pl.num_programs(1)
    c = n_chunks - 1 - i

    @pl.when(i == 0)
    def _():
        carry_scr[...] = jnp.zeros_like(carry_scr)

    half_decay = _half_log2_decay(lam_ref)
    w = LRU_BLOCK_W
    for n in range(LRU_BLOCKS):
        sl = slice(n * w, (n + 1) * w)
        xh_bf16 = xs_ref[:, sl]
        _gate_scan_block(n, True, xh_bf16.astype(F32), xh_bf16, gw_ref, half_decay, carry_scr.at[pl.ds(n, 1)],
                         h_scr.at[n])
        rec = hf_ref[:, sl].astype(F32) + h_scr[n]
        z_scr[:, sl] = (rec * y_ref[:, sl].astype(F32)).astype(BF16)
    m = jnp.dot(z_scr[...], wout_ref[...], preferred_element_type=F32)
    out_ref[...] = _stream_rows(h_refs, c) + _rms(m, g_ref[...])


def _gate_spec():
    return _const_spec((LRU_BLOCKS, 2 * LRU_BLOCK_W, 2 * LRU_BLOCK_W))


def _lru_fwd(xb, cw, cb, gw, lam):
    B, L, D = xb.shape
    tl = SCAN_TILE
    per = tl // HALO
    last = L // HALO - 1
    cur = _tok_spec(tl, D)
    prev = pl.BlockSpec((None, HALO, D), lambda b, i: (b, jnp.maximum(i * per - 1, 0), 0))
    nxt = pl.BlockSpec((None, HALO, D), lambda b, i: (b, jnp.minimum((i + 1) * per, last), 0))
    out = jax.ShapeDtypeStruct((B, L, D), BF16)
    return pl.pallas_call(
        _lru_fwd_kernel,
        out_shape=(out, out),
        grid=(B, L // tl),
        in_specs=[prev, cur, nxt, _const_spec((4, D)), _const_spec((1, D)), _gate_spec(), _const_spec((1, D))],
        out_specs=(cur, cur),
        scratch_shapes=[pltpu.VMEM((LRU_BLOCKS, tl + 16, LRU_BLOCK_W), F32),
                        pltpu.VMEM((LRU_BLOCKS, tl, LRU_BLOCK_W), F32),
                        pltpu.VMEM((LRU_BLOCKS, LRU_BLOCK_W), F32)],
        compiler_params=_params("parallel", "arbitrary"),
        name="lru_fwd",
    )(xb, xb, xb, cw, cb, gw, lam)


def _lru_bwd(xs, gw, lam, hf, y, h, w_out, g):
    B, L, D = xs.shape
    tl = SCAN_TILE
    n_chunks = L // tl
    chunk_of = lambda i: n_chunks - 1 - i
    cur = pl.BlockSpec((None, tl, D), lambda b, i: (b, chunk_of(i), 0))
    h_ops, h_specs = _stream_specs(h, tl, chunk_of)
    return pl.pallas_call(
        _lru_bwd_kernel,
        out_shape=jax.ShapeDtypeStruct((B, L, D), F32),
        grid=(B, n_chunks),
        in_specs=[cur, _gate_spec(), _const_spec((1, D)), cur, cur, _const_spec((D, D)),
                  _const_spec((1, D))] + h_specs,
        out_specs=cur,
        scratch_shapes=[pltpu.VMEM((LRU_BLOCKS, tl, LRU_BLOCK_W), F32),
                        pltpu.VMEM((LRU_BLOCKS, LRU_BLOCK_W), F32),
                        pltpu.VMEM((tl, D), BF16)],
        compiler_params=_params("parallel", "arbitrary"),
        name="lru_bwd",
    )(xs, gw, lam, hf, y, w_out, g, *h_ops)


def _gate_weights(gate_w, gate_b):
    w = LRU_BLOCK_W
    wts = jnp.concatenate([gate_w[:, 0], gate_w[:, 1]], axis=-1)
    bias = 0.5 * gate_b.reshape(2, 2, LRU_BLOCKS, w).transpose(0, 2, 1, 3).reshape(2, LRU_BLOCKS, 1, 2 * w)
    hi = bias.astype(BF16)
    lo = (bias - hi.astype(F32)).astype(BF16)
    pad = jnp.zeros((2, LRU_BLOCKS, w - 2, 2 * w), BF16)
    return jnp.concatenate([wts.astype(BF16), hi, lo, pad], axis=2)


def _rope(x, cos, sin_signed):
    return x * cos + pltpu.roll(x, HEAD_DIM // 2, 1) * sin_signed


def _qkv_kernel(h_ref, g_ref, w_ref, qg_ref, kg_ref, cos_ref, sin_ref, q_ref, k_ref, v_ref):
    hd = HEAD_DIM
    nq = N_HEADS * hd
    nk = N_KV_HEADS * hd
    q_gain = qg_ref[...] * Q_SCALE
    sub = h_ref.shape[0] // QKV_SUBTILES
    for s in range(QKV_SUBTILES):
        rows = pl.ds(s * sub, sub)
        u = _rms(h_ref[rows, :], g_ref[...]).astype(BF16)
        cos = cos_ref[rows, :]
        sin = sin_ref[rows, :]
        q = jnp.dot(u, w_ref[:, :nq], preferred_element_type=F32)
        for hh in range(N_HEADS):
            sl = slice(hh * hd, (hh + 1) * hd)
            q_ref[rows, sl] = _rope(_rms(q[:, sl], q_gain), cos, sin).astype(BF16)
        kv = jnp.dot(u, w_ref[:, nq:], preferred_element_type=F32)
        for hh in range(N_KV_HEADS):
            sl = slice(hh * hd, (hh + 1) * hd)
            k_ref[rows, sl] = _rope(_rms(kv[:, sl], kg_ref[...]), cos, sin).astype(BF16)
        v_ref[rows, :] = kv[:, nk:].astype(BF16)


def _qkv(h, g, w, qg, kg, cos, sin):
    B, L, D = h.shape
    nq = N_HEADS * HEAD_DIM
    nk = N_KV_HEADS * HEAD_DIM
    tab = pl.BlockSpec((TOKEN_TILE, HEAD_DIM), lambda b, i: (i, 0))
    return pl.pallas_call(
        _qkv_kernel,
        out_shape=(jax.ShapeDtypeStruct((B, L, nq), BF16), jax.ShapeDtypeStruct((B, L, nk), BF16),
                   jax.ShapeDtypeStruct((B, L, nk), BF16)),
        grid=(B, L // TOKEN_TILE),
        in_specs=[_tok_spec(TOKEN_TILE, D), _const_spec((1, D)), _const_spec((D, nq + 2 * nk)),
                  _const_spec((1, HEAD_DIM)), _const_spec((1, HEAD_DIM)), tab, tab],
        out_specs=(_tok_spec(TOKEN_TILE, nq), _tok_spec(TOKEN_TILE, nk), _tok_spec(TOKEN_TILE, nk)),
        compiler_params=_params("parallel", "parallel"),
        name="qkv",
    )(h, g, w, qg, kg, cos, sin)


def _scores(q, kc):
    return lax.dot_general(q, kc, (((1,), (1,)), ((), ())), preferred_element_type=F32)


def _attn_bounded_kernel(q_ref, k_ref, vt_ref, o_ref, *, key_chunk, unroll):
    tq = q_ref.shape[0]
    L = k_ref.shape[0]
    hd = HEAD_DIM
    qs = [q_ref[:, g * hd:(g + 1) * hd] for g in range(GQA_GROUP)]

    def step(kc, vtc, carry):
        out = []
        for g, (acc, den) in enumerate(carry):
            p = jnp.exp2(_scores(kc, qs[g]))
            den = den + p.reshape(-1, 8, tq).sum(axis=0)
            out.append((acc + jnp.dot(vtc, p.astype(BF16), preferred_element_type=F32), den))
        return tuple(out)

    n_full = L // key_chunk
    tail = L - n_full * key_chunk
    carry = tuple((jnp.zeros((hd, tq), F32), jnp.zeros((8, tq), F32)) for _ in range(GQA_GROUP))
    if tail:
        carry = step(k_ref[pl.ds(n_full * key_chunk, tail), :], vt_ref[:, pl.ds(n_full * key_chunk, tail)], carry)

    def body(j, carry):
        k0 = pl.multiple_of(j * key_chunk, key_chunk)
        return step(k_ref[pl.ds(k0, key_chunk), :], vt_ref[:, pl.ds(k0, key_chunk)], carry)

    carry = lax.fori_loop(0, n_full, body, carry, unroll=unroll)
    for g in range(GQA_GROUP):
        acc, den = carry[g]
        out_t = acc * (1.0 / jnp.sum(den, axis=0, keepdims=True))
        o_ref[:, g * hd:(g + 1) * hd] = out_t.T.astype(BF16)


def _attn_online_kernel(q_ref, k_ref, v_ref, o_ref, *, key_chunk, unroll):
    tq = q_ref.shape[0]
    L = k_ref.shape[0]
    hd = HEAD_DIM
    qs = [q_ref[:, g * hd:(g + 1) * hd] for g in range(GQA_GROUP)]

    def step(kc, vc, carry):
        out = []
        for g in range(GQA_GROUP):
            m, l, acc = carry[g]
            s = _scores(qs[g], kc)
            m_new = jnp.maximum(m, jnp.max(s, axis=-1, keepdims=True))
            alpha = jnp.exp2(m - m_new)
            p = jnp.exp2(s - m_new)
            l = alpha * l + jnp.sum(p, axis=-1, keepdims=True)
            acc = alpha * acc + jnp.dot(p.astype(BF16), vc, preferred_element_type=F32)
            out.append((m_new, l, acc))
        return tuple(out)

    n_full = L // key_chunk
    tail = L - n_full * key_chunk
    carry = tuple((jnp.full((tq, 1), -jnp.inf, F32), jnp.zeros((tq, 1), F32), jnp.zeros((tq, hd), F32))
                  for _ in range(GQA_GROUP))
    if tail:
        carry = step(k_ref[pl.ds(n_full * key_chunk, tail), :], v_ref[pl.ds(n_full * key_chunk, tail), :],
                     carry)

    def body(j, carry):
        k0 = pl.multiple_of(j * key_chunk, key_chunk)
        return step(k_ref[pl.ds(k0, key_chunk), :], v_ref[pl.ds(k0, key_chunk), :], carry)

    carry = lax.fori_loop(0, n_full, body, carry, unroll=unroll)
    for g in range(GQA_GROUP):
        _, l, acc = carry[g]
        o_ref[:, g * hd:(g + 1) * hd] = (acc * (1.0 / l)).astype(BF16)


def _attention(q, k, v, *, bounded):
    B, L, nq = q.shape
    gw = GQA_GROUP * HEAD_DIM
    q_tile, key_chunk, unroll = FAST_ATTN_TILES if bounded else ONLINE_ATTN_TILES
    q_spec = pl.BlockSpec((None, q_tile, gw), lambda b, kh, i: (b, i, kh))
    kv_spec = pl.BlockSpec((None, L, HEAD_DIM), lambda b, kh, i: (b, 0, kh))
    if bounded:
        v_in = v.reshape(B, L, N_KV_HEADS, HEAD_DIM).transpose(0, 2, 3, 1)
        v_spec = pl.BlockSpec((None, None, HEAD_DIM, L), lambda b, kh, i: (b, kh, 0, 0))
    else:
        v_in, v_spec = v, kv_spec
    return pl.pallas_call(
        functools.partial(_attn_bounded_kernel if bounded else _attn_online_kernel, key_chunk=key_chunk,
                          unroll=unroll),
        out_shape=jax.ShapeDtypeStruct((B, L, nq), BF16),
        grid=(B, N_KV_HEADS, L // q_tile),
        in_specs=[q_spec, kv_spec, v_spec],
        out_specs=q_spec,
        compiler_params=_params("parallel", "parallel", "parallel"),
        name="attention_bounded" if bounded else "attention_online",
    )(q, k, v_in)


def _ffn_kernel(*refs, with_mixer_out):
    if with_mixer_out:
        z_ref, w_mix_ref, g_mix_ref, h_ref, g_pre_ref, w_in_ref, w_out_ref, g_post_ref, out_ref = refs
        m = jnp.dot(z_ref[...], w_mix_ref[...], preferred_element_type=F32)
        h = h_ref[...] + _rms(m, g_mix_ref[...])
    else:
        h_ref, g_pre_ref, w_in_ref, w_out_ref, g_post_ref, out_ref = refs
        h = h_ref[...]
    u = _rms(h, g_pre_ref[...]).astype(BF16)
    d_ff = w_out_ref.shape[0]
    acc = None
    split = (d_ff // MXU_DIM + 1) // 2 * MXU_DIM
    for c0, c1 in ((0, split), (split, d_ff)):
        gate = jnp.dot(u, w_in_ref[:, c0:c1], preferred_element_type=F32)
        up = jnp.dot(u, w_in_ref[:, d_ff + c0:d_ff + c1], preferred_element_type=F32)
        act = (gate * _sigmoid(gate) * up).astype(BF16)
        part = jnp.dot(act, w_out_ref[c0:c1, :], preferred_element_type=F32)
        acc = part if acc is None else acc + part
    out_ref[...] = h + _rms(acc, g_post_ref[...])


def _ffn(h, g_pre, w_in, w_out, g_post, mixer_out=None, drop_leading=0):
    B, L, D = h.shape
    n_out = L - drop_leading
    if drop_leading:
        tile = FINAL_TILE
        flat = lambda a: a.reshape(B * L, a.shape[-1])
        tok = lambda width: pl.BlockSpec((pl.Element(tile), pl.Element(width)),
                                         lambda b, i: (pl.multiple_of(b * L + drop_leading + i * tile, BF16_ROWS), 0))
        out_shape = jax.ShapeDtypeStruct((B * n_out, D), F32)
        out_spec = pl.BlockSpec((tile, D), lambda b, i: (b * (n_out // tile) + i, 0))
    else:
        tile = TOKEN_TILE
        flat = lambda a: a
        tok = lambda width: _tok_spec(tile, width)
        out_shape = jax.ShapeDtypeStruct((B, L, D), F32)
        out_spec = _tok_spec(tile, D)
    args = [flat(h), g_pre, w_in, w_out, g_post]
    in_specs = [tok(D), _const_spec((1, D)), _const_spec(w_in.shape), _const_spec(w_out.shape),
                _const_spec((1, D))]
    if mixer_out is not None:
        z, w_mix, g_mix = mixer_out
        args = [flat(z), w_mix, g_mix] + args
        in_specs = [tok(z.shape[-1]), _const_spec(w_mix.shape), _const_spec((1, D))] + in_specs
    out = pl.pallas_call(
        functools.partial(_ffn_kernel, with_mixer_out=mixer_out is not None),
        out_shape=out_shape,
        grid=(B, n_out // tile),
        in_specs=in_specs,
        out_specs=out_spec,
        compiler_params=_params("parallel", "parallel"),
        name="ffn_after_attention" if mixer_out is not None else "ffn",
    )(*args)
    return out.reshape(B, n_out, D)


def _rope_tables(n_tokens):
    rows = n_tokens // GRID_W
    inv_freq = ROPE_THETA ** (-jnp.arange(0, 2 * ROPE_PAIRS, 2, dtype=F32) / (2 * ROPE_PAIRS))
    ang_r = jnp.arange(rows, dtype=F32)[:, None] * inv_freq
    ang_c = jnp.arange(GRID_W, dtype=F32)[:, None] * inv_freq
    ang_r = jnp.broadcast_to(ang_r[:, None, :], (rows, GRID_W, ROPE_PAIRS)).reshape(n_tokens, ROPE_PAIRS)
    ang_c = jnp.broadcast_to(ang_c[None, :, :], (rows, GRID_W, ROPE_PAIRS)).reshape(n_tokens, ROPE_PAIRS)
    ang = jnp.concatenate([ang_r, ang_c, ang_r, ang_c], axis=-1)
    ang = jnp.concatenate([jnp.zeros((N_META, HEAD_DIM), F32), ang], axis=0)
    sign = jnp.concatenate([-jnp.ones((2 * ROPE_PAIRS,), F32), jnp.ones((2 * ROPE_PAIRS,), F32)])
    return jnp.cos(ang), jnp.sin(ang) * sign


def _head_dim_order():
    p = ROPE_PAIRS
    return jnp.concatenate([jnp.arange(0, p), jnp.arange(2 * p, 3 * p), jnp.arange(p, 2 * p),
                            jnp.arange(3 * p, 4 * p)])


def _reorder_qk_columns(w_qkv):
    order = _head_dim_order()
    n_qk = (N_HEADS + N_KV_HEADS) * HEAD_DIM
    cols = (jnp.arange(n_qk).reshape(-1, HEAD_DIM)[:, :1] + order[None, :]).reshape(-1)
    return jnp.concatenate([w_qkv[:, cols], w_qkv[:, n_qk:]], axis=1)


def kernel(x, meta_tokens, norm_gains, lru_w_in, lru_conv_w, lru_conv_b, lru_gate_w, lru_gate_b,
           lru_lambda, lru_w_out, attn_w_qkv, attn_q_gain, attn_k_gain, attn_w_out, ffn_w_in,
           ffn_w_out):
    B, S, D = x.shape
    depth = norm_gains.shape[0]
    L = N_META + S
    h = (x, meta_tokens.astype(x.dtype))
    cos, sin = _rope_tables(S)
    gains = norm_gains.reshape(depth, 4, 1, D)
    for layer in range(depth):
        g = gains[layer]
        slot = layer // 2
        mixer_out = None
        if layer % 2 == 0:
            cw = lru_conv_w[slot]
            cb = lru_conv_b[slot].reshape(1, D)
            gw = _gate_weights(lru_gate_w[slot], lru_gate_b[slot])
            lam = lru_lambda[slot].reshape(2, 1, D)
            y, xb = _lru_in(h, g[0], lru_w_in[slot].astype(BF16), B, L)
            hf, xs = _lru_fwd(xb, cw, cb, gw[0], lam[0])
            h = _lru_bwd(xs, gw[1], lam[1], hf, y, h, lru_w_out[slot].astype(BF16), g[1])
        else:
            order = _head_dim_order()
            q, k, v = _qkv(h, g[0], _reorder_qk_columns(attn_w_qkv[slot]).astype(BF16),
                           attn_q_gain[slot][order].reshape(1, HEAD_DIM),
                           attn_k_gain[slot][order].reshape(1, HEAD_DIM), cos, sin)
            score_bound = (HEAD_DIM * Q_SCALE * jnp.max(jnp.abs(attn_q_gain[slot]))
                           * jnp.max(jnp.abs(attn_k_gain[slot])))
            o = lax.cond(score_bound <= SCORE_BOUND_MAX,
                         functools.partial(_attention, bounded=True),
                         functools.partial(_attention, bounded=False), q, k, v)
            mixer_out = (o, attn_w_out[slot].astype(BF16), g[1])
        h = _ffn(h, g[2], ffn_w_in[layer].astype(BF16), ffn_w_out[layer].astype(BF16), g[3], mixer_out,
                 drop_leading=N_META if layer == depth - 1 else 0)
    return h
```

```python
import functools

import jax
import jax.numpy as jnp
from jax import lax
from jax.experimental import pallas as pl
from jax.experimental.pallas import tpu as pltpu

F32 = jnp.float32
BF16 = jnp.bfloat16

D_MODEL = 1024
N_META = 16
GRID_W = 64
RMS_EPS = 1e-6
LRU_BLOCKS = 8
LRU_BLOCK_W = 128
LRU_C = 8.0
LOG2_E = 1.4426950408889634
F32_TINY = 1.1754943508222875e-38
HEAD_DIM = 128
N_HEADS = 8
N_KV_HEADS = 2
GQA_GROUP = 4
ROPE_PAIRS = 32
ROPE_THETA = 10000.0

VMEM_LIMIT_BYTES = 56 * 1024 * 1024

TOKEN_TILE = 912
SCAN_TILE = 912
FAST_ATTN_TILES = (2736, 512, 4)
ONLINE_ATTN_TILES = (912, 1024, 2)
FINAL_TILE = 1024
QKV_SUBTILES = 3
MXU_DIM = 256
Q_SCALE = HEAD_DIM ** -0.5 * LOG2_E
SCORE_BOUND_MAX = 80.0
BF16_ROWS = 16
HALO = BF16_ROWS


def _params(*sem):
    return pltpu.CompilerParams(dimension_semantics=sem, vmem_limit_bytes=VMEM_LIMIT_BYTES)


def _tok_spec(tile, width):
    return pl.BlockSpec((None, tile, width), lambda b, i: (b, i, 0))


def _const_spec(shape):
    zeros = (0,) * len(shape)
    return pl.BlockSpec(shape, lambda b, i: zeros, pipeline_mode=pl.Buffered(1))


def _rms(x, g):
    return x * lax.rsqrt(jnp.mean(x * x, axis=-1, keepdims=True) + RMS_EPS) * g


def _sigmoid(x):
    return 0.5 * jnp.tanh(0.5 * x) + 0.5


def _gelu_tanh(x):
    return 0.5 * x * (1.0 + jnp.tanh(0.7978845608028654 * (x + 0.044715 * (x * x * x))))


def _stream_specs(h, tile, chunk_of=lambda i: i):
    if isinstance(h, tuple):
        x, meta = h
        B, S, D = x.shape
        start = lambda b, i: pl.multiple_of(b * S + jnp.maximum(chunk_of(i) * tile - N_META, 0), 8)
        return ([x.reshape(B * S, D), meta],
                [pl.BlockSpec((pl.Element(tile), pl.Element(D)), lambda b, i: (start(b, i), 0)),
                 _const_spec(meta.shape)])
    return [h], [pl.BlockSpec((None, tile, h.shape[-1]), lambda b, i: (b, chunk_of(i), 0))]


def _stream_rows(refs, chunk):
    if len(refs) == 1:
        return refs[0][...]
    x_ref, meta_ref = refs
    xblk = x_ref[...]
    first = jnp.concatenate([meta_ref[...], xblk[:xblk.shape[0] - N_META, :]], axis=0)
    return jnp.where(chunk == 0, first, xblk)


def _lru_in_kernel(*refs):
    *h_refs, g_ref, w_ref, y_ref, xb_ref = refs
    u = _rms(_stream_rows(h_refs, pl.program_id(1)), g_ref[...]).astype(BF16)
    d = y_ref.shape[-1]
    y_ref[...] = _gelu_tanh(jnp.dot(u, w_ref[:, :d], preferred_element_type=F32)).astype(BF16)
    xb_ref[...] = jnp.dot(u, w_ref[:, d:], preferred_element_type=F32).astype(BF16)


def _lru_in(h, g, w, B, L):
    D = w.shape[0]
    out = jax.ShapeDtypeStruct((B, L, D), BF16)
    h_ops, h_specs = _stream_specs(h, TOKEN_TILE)
    return pl.pallas_call(
        _lru_in_kernel,
        out_shape=(out, out),
        grid=(B, L // TOKEN_TILE),
        in_specs=h_specs + [_const_spec((1, D)), _const_spec((D, 2 * D))],
        out_specs=(_tok_spec(TOKEN_TILE, D), _tok_spec(TOKEN_TILE, D)),
        compiler_params=_params("parallel", "parallel"),
        name="lru_in",
    )(*h_ops, g, w)


def _fill_xpad(c, n_chunks, xprev_ref, xcur_ref, xnext_ref, xpad_scr):
    tl = xcur_ref.shape[0]
    w = LRU_BLOCK_W
    prev = jnp.where(c > 0, xprev_ref[HALO - 8:, :].astype(F32), 0.0)
    nxt = jnp.where(c < n_chunks - 1, xnext_ref[:8, :].astype(F32), 0.0)
    for n in range(LRU_BLOCKS):
        sl = slice(n * w, (n + 1) * w)
        xpad_scr[n, pl.ds(0, 8), :] = prev[:, sl]
        xpad_scr[n, pl.ds(8, tl), :] = xcur_ref[:, sl].astype(F32)
        xpad_scr[n, pl.ds(8 + tl, 8), :] = nxt[:, sl]


def _half_log2_decay(lam_ref):
    x = -lam_ref[...]
    e = jnp.exp(-jnp.abs(x))
    u1 = 1.0 + e
    log1p_e = jnp.where(u1 == 1.0, e, jnp.log(u1) * (e / (u1 - 1.0)))
    return (-0.5 * LRU_C * LOG2_E) * (jnp.maximum(x, 0.0) + log1p_e)


def _conv_block(n, xpad_scr, cw_ref, cb_ref):
    w = LRU_BLOCK_W
    sl = slice(n * w, (n + 1) * w)
    seg = (xpad_scr.shape[1] - 16) // 8
    cb = 0.5 * cb_ref[:, sl]
    taps = [0.5 * cw_ref[k:k + 1, sl] for k in range(4)]
    tiles = []
    for r in range(seg):
        t = cb
        for k in range(4):
            t = t + xpad_scr[n, pl.ds(6 + r + k, 8, stride=seg), :] * taps[k]
        tiles.append(t)
    return jnp.concatenate(tiles, axis=0)


def _gate_scan_block(n, reverse, xh, xh_bf16, gw_ref, half_decay, carry_scr, h_scr):
    w = LRU_BLOCK_W
    sl = slice(n * w, (n + 1) * w)
    seg = xh.shape[0] // 8
    ones = (lax.broadcasted_iota(jnp.int32, xh.shape, 1) < 2).astype(BF16)
    gp = jnp.dot(jnp.concatenate([xh_bf16, ones], axis=-1), gw_ref[n],
                 preferred_element_type=F32)
    t_r = jnp.tanh(gp[:, :w])
    t_i = jnp.tanh(gp[:, w:])
    hd = half_decay[:, sl]
    log2_a = t_r * hd + hd
    a = jnp.exp2(log2_a)
    one_minus_a2 = jnp.tanh(log2_a * (-1.0 / LOG2_E)) * (1.0 + a * a)
    root = one_minus_a2 * lax.rsqrt(jnp.maximum(one_minus_a2, F32_TINY))
    b = root * (t_i * xh + xh)

    steps = range(seg - 1, -1, -1) if reverse else range(seg)
    hs = [None] * seg
    ps = [None] * seg
    h_prev = p_prev = None
    for r in steps:
        a_r = a[r * 8:(r + 1) * 8, :]
        b_r = b[r * 8:(r + 1) * 8, :]
        hs[r] = b_r if h_prev is None else a_r * h_prev + b_r
        ps[r] = a_r if p_prev is None else a_r * p_prev
        h_prev, p_prev = hs[r], ps[r]
    carry = carry_scr[n:n + 1, :]
    seg_in = [None] * 8
    for j in (range(7, -1, -1) if reverse else range(8)):
        seg_in[j] = carry
        carry = h_prev[j:j + 1, :] + p_prev[j:j + 1, :] * carry
    carry_scr[n:n + 1, :] = carry
    seg_in = jnp.concatenate(seg_in, axis=0)
    for r in range(seg):
        h_scr[n, pl.ds(r, 8, stride=seg), :] = hs[r] + ps[r] * seg_in


def _lru_fwd_kernel(xprev_ref, xcur_ref, xnext_ref, cw_ref, cb_ref, gw_ref, lam_ref,
                    hf_ref, xs_ref, xpad_scr, h_scr, carry_scr):
    c = pl.program_id(1)
    n_chunks = pl.num_programs(1)

    @pl.when(c == 0)
    def _():
        carry_scr[...] = jnp.zeros_like(carry_scr)

    _fill_xpad(c, n_chunks, xprev_ref, xcur_ref, xnext_ref, xpad_scr)
    half_decay = _half_log2_decay(lam_ref)
    w = LRU_BLOCK_W
    for n in range(LRU_BLOCKS):
        sl = slice(n * w, (n + 1) * w)
        xh = _conv_block(n, xpad_scr, cw_ref, cb_ref)
        xh_bf16 = xh.astype(BF16)
        xs_ref[:, sl] = xh_bf16
        _gate_scan_block(n, False, xh, xh_bf16, gw_ref, half_decay, carry_scr, h_scr)
        hf_ref[:, sl] = h_scr[n].astype(BF16)


def _lru_bwd_kernel(xs_ref, gw_ref, lam_ref, hf_ref, y_ref, wout_ref, g_ref, *rest):
    *h_refs, out_ref, h_scr, carry_scr, z_scr = rest
    i = pl.program_id(1)
    n_chunks = pl.num_programs(1)
    c = n_chunks - 1 - i

    @pl.when(i == 0)
    def _():
        carry_scr[...] = jnp.zeros_like(carry_scr)

    half_decay = _half_log2_decay(lam_ref)
    w = LRU_BLOCK_W
    for n in range(LRU_BLOCKS):
        sl = slice(n * w, (n + 1) * w)
        xh_bf16 = xs_ref[:, sl]
        _gate_scan_block(n, True, xh_bf16.astype(F32), xh_bf16, gw_ref, half_decay, carry_scr, h_scr)
        rec = hf_ref[:, sl].astype(F32) + h_scr[n]
        z_scr[:, sl] = (rec * y_ref[:, sl].astype(F32)).astype(BF16)
    m = jnp.dot(z_scr[...], wout_ref[...], preferred_element_type=F32)
    out_ref[...] = _stream_rows(h_refs, c) + _rms(m, g_ref[...])


def _gate_spec():
    return _const_spec((LRU_BLOCKS, 2 * LRU_BLOCK_W, 2 * LRU_BLOCK_W))


def _lru_fwd(xb, cw, cb, gw, lam):
    B, L, D = xb.shape
    tl = SCAN_TILE
    per = tl // HALO
    last = L // HALO - 1
    cur = _tok_spec(tl, D)
    prev = pl.BlockSpec((None, HALO, D), lambda b, i: (b, jnp.maximum(i * per - 1, 0), 0))
    nxt = pl.BlockSpec((None, HALO, D), lambda b, i: (b, jnp.minimum((i + 1) * per, last), 0))
    out = jax.ShapeDtypeStruct((B, L, D), BF16)
    return pl.pallas_call(
        _lru_fwd_kernel,
        out_shape=(out, out),
        grid=(B, L // tl),
        in_specs=[prev, cur, nxt, _const_spec((4, D)), _const_spec((1, D)), _gate_spec(), _const_spec((1, D))],
        out_specs=(cur, cur),
        scratch_shapes=[pltpu.VMEM((LRU_BLOCKS, tl + 16, LRU_BLOCK_W), F32),
                        pltpu.VMEM((LRU_BLOCKS, tl, LRU_BLOCK_W), F32),
                        pltpu.VMEM((LRU_BLOCKS, LRU_BLOCK_W), F32)],
        compiler_params=_params("parallel", "arbitrary"),
        name="lru_fwd",
    )(xb, xb, xb, cw, cb, gw, lam)


def _lru_bwd(xs, gw, lam, hf, y, h, w_out, g):
    B, L, D = xs.shape
    tl = SCAN_TILE
    n_chunks = L // tl
    chunk_of = lambda i: n_chunks - 1 - i
    cur = pl.BlockSpec((None, tl, D), lambda b, i: (b, chunk_of(i), 0))
    h_ops, h_specs = _stream_specs(h, tl, chunk_of)
    return pl.pallas_call(
        _lru_bwd_kernel,
        out_shape=jax.ShapeDtypeStruct((B, L, D), F32),
        grid=(B, n_chunks),
        in_specs=[cur, _gate_spec(), _const_spec((1, D)), cur, cur, _const_spec((D, D)),
                  _const_spec((1, D))] + h_specs,
        out_specs=cur,
        scratch_shapes=[pltpu.VMEM((LRU_BLOCKS, tl, LRU_BLOCK_W), F32),
                        pltpu.VMEM((LRU_BLOCKS, LRU_BLOCK_W), F32),
                        pltpu.VMEM((tl, D), BF16)],
        compiler_params=_params("parallel", "arbitrary"),
        name="lru_bwd",
    )(xs, gw, lam, hf, y, w_out, g, *h_ops)


def _gate_weights(gate_w, gate_b):
    w = LRU_BLOCK_W
    wts = jnp.concatenate([gate_w[:, 0], gate_w[:, 1]], axis=-1)
    bias = 0.5 * gate_b.reshape(2, 2, LRU_BLOCKS, w).transpose(0, 2, 1, 3).reshape(2, LRU_BLOCKS, 1, 2 * w)
    hi = bias.astype(BF16)
    lo = (bias - hi.astype(F32)).astype(BF16)
    pad = jnp.zeros((2, LRU_BLOCKS, w - 2, 2 * w), BF16)
    return jnp.concatenate([wts.astype(BF16), hi, lo, pad], axis=2)


def _rope(x, cos, sin_signed):
    return x * cos + pltpu.roll(x, HEAD_DIM // 2, 1) * sin_signed


def _qkv_kernel(h_ref, g_ref, w_ref, qg_ref, kg_ref, cos_ref, sin_ref, q_ref, k_ref, v_ref):
    hd = HEAD_DIM
    nq = N_HEADS * hd
    nk = N_KV_HEADS * hd
    q_gain = qg_ref[...] * Q_SCALE
    sub = h_ref.shape[0] // QKV_SUBTILES
    for s in range(QKV_SUBTILES):
        rows = pl.ds(s * sub, sub)
        u = _rms(h_ref[rows, :], g_ref[...]).astype(BF16)
        cos = cos_ref[rows, :]
        sin = sin_ref[rows, :]
        q = jnp.dot(u, w_ref[:, :nq], preferred_element_type=F32)
        for hh in range(N_HEADS):
            sl = slice(hh * hd, (hh + 1) * hd)
            q_ref[rows, sl] = _rope(_rms(q[:, sl], q_gain), cos, sin).astype(BF16)
        kv = jnp.dot(u, w_ref[:, nq:], preferred_element_type=F32)
        for hh in range(N_KV_HEADS):
            sl = slice(hh * hd, (hh + 1) * hd)
            k_ref[rows, sl] = _rope(_rms(kv[:, sl], kg_ref[...]), cos, sin).astype(BF16)
        v_ref[rows, :] = kv[:, nk:].astype(BF16)


def _qkv(h, g, w, qg, kg, cos, sin):
    B, L, D = h.shape
    nq = N_HEADS * HEAD_DIM
    nk = N_KV_HEADS * HEAD_DIM
    tab = pl.BlockSpec((TOKEN_TILE, HEAD_DIM), lambda b, i: (i, 0))
    return pl.pallas_call(
        _qkv_kernel,
        out_shape=(jax.ShapeDtypeStruct((B, L, nq), BF16), jax.ShapeDtypeStruct((B, L, nk), BF16),
                   jax.ShapeDtypeStruct((B, L, nk), BF16)),
        grid=(B, L // TOKEN_TILE),
        in_specs=[_tok_spec(TOKEN_TILE, D), _const_spec((1, D)), _const_spec((D, nq + 2 * nk)),
                  _const_spec((1, HEAD_DIM)), _const_spec((1, HEAD_DIM)), tab, tab],
        out_specs=(_tok_spec(TOKEN_TILE, nq), _tok_spec(TOKEN_TILE, nk), _tok_spec(TOKEN_TILE, nk)),
        compiler_params=_params("parallel", "parallel"),
        name="qkv",
    )(h, g, w, qg, kg, cos, sin)


def _scores(q, kc):
    return lax.dot_general(q, kc, (((1,), (1,)), ((), ())), preferred_element_type=F32)


def _attn_bounded_kernel(q_ref, k_ref, vt_ref, o_ref, *, key_chunk, unroll):
    tq = q_ref.shape[0]
    L = k_ref.shape[0]
    hd = HEAD_DIM
    qs = [q_ref[:, g * hd:(g + 1) * hd] for g in range(GQA_GROUP)]

    def step(kc, vtc, carry):
        out = []
        for g, (acc, den) in enumerate(carry):
            p = jnp.exp2(_scores(kc, qs[g]))
            den = den + p.reshape(-1, 8, tq).sum(axis=0)
            out.append((acc + jnp.dot(vtc, p.astype(BF16), preferred_element_type=F32), den))
        return tuple(out)

    n_full = L // key_chunk
    tail = L - n_full * key_chunk
    carry = tuple((jnp.zeros((hd, tq), F32), jnp.zeros((8, tq), F32)) for _ in range(GQA_GROUP))
    if tail:
        carry = step(k_ref[pl.ds(n_full * key_chunk, tail), :], vt_ref[:, pl.ds(n_full * key_chunk, tail)], carry)

    def body(j, carry):
        k0 = pl.multiple_of(j * key_chunk, key_chunk)
        return step(k_ref[pl.ds(k0, key_chunk), :], vt_ref[:, pl.ds(k0, key_chunk)], carry)

    carry = lax.fori_loop(0, n_full, body, carry, unroll=unroll)
    for g in range(GQA_GROUP):
        acc, den = carry[g]
        out_t = acc * (1.0 / jnp.sum(den, axis=0, keepdims=True))
        o_ref[:, g * hd:(g + 1) * hd] = out_t.T.astype(BF16)


def _attn_online_kernel(q_ref, k_ref, v_ref, o_ref, *, key_chunk, unroll):
    tq = q_ref.shape[0]
    L = k_ref.shape[0]
    hd = HEAD_DIM
    qs = [q_ref[:, g * hd:(g + 1) * hd] for g in range(GQA_GROUP)]

    def step(kc, vc, carry):
        out = []
        for g in range(GQA_GROUP):
            m, l, acc = carry[g]
            s = _scores(qs[g], kc)
            m_new = jnp.maximum(m, jnp.max(s, axis=-1, keepdims=True))
            alpha = jnp.exp2(m - m_new)
            p = jnp.exp2(s - m_new)
            l = alpha * l + jnp.sum(p, axis=-1, keepdims=True)
            acc = alpha * acc + jnp.dot(p.astype(BF16), vc, preferred_element_type=F32)
            out.append((m_new, l, acc))
        return tuple(out)

    n_full = L // key_chunk
    tail = L - n_full * key_chunk
    carry = tuple((jnp.full((tq, 1), -jnp.inf, F32), jnp.zeros((tq, 1), F32), jnp.zeros((tq, hd), F32))
                  for _ in range(GQA_GROUP))
    if tail:
        carry = step(k_ref[pl.ds(n_full * key_chunk, tail), :], v_ref[pl.ds(n_full * key_chunk, tail), :],
                     carry)

    def body(j, carry):
        k0 = pl.multiple_of(j * key_chunk, key_chunk)
        return step(k_ref[pl.ds(k0, key_chunk), :], v_ref[pl.ds(k0, key_chunk), :], carry)

    carry = lax.fori_loop(0, n_full, body, carry, unroll=unroll)
    for g in range(GQA_GROUP):
        _, l, acc = carry[g]
        o_ref[:, g * hd:(g + 1) * hd] = (acc * (1.0 / l)).astype(BF16)


def _attention(q, k, v, *, bounded):
    B, L, nq = q.shape
    gw = GQA_GROUP * HEAD_DIM
    q_tile, key_chunk, unroll = FAST_ATTN_TILES if bounded else ONLINE_ATTN_TILES
    q_spec = pl.BlockSpec((None, q_tile, gw), lambda b, kh, i: (b, i, kh))
    kv_spec = pl.BlockSpec((None, L, HEAD_DIM), lambda b, kh, i: (b, 0, kh))
    if bounded:
        v_in = v.reshape(B, L, N_KV_HEADS, HEAD_DIM).transpose(0, 2, 3, 1)
        v_spec = pl.BlockSpec((None, None, HEAD_DIM, L), lambda b, kh, i: (b, kh, 0, 0))
    else:
        v_in, v_spec = v, kv_spec
    return pl.pallas_call(
        functools.partial(_attn_bounded_kernel if bounded else _attn_online_kernel, key_chunk=key_chunk,
                          unroll=unroll),
        out_shape=jax.ShapeDtypeStruct((B, L, nq), BF16),
        grid=(B, N_KV_HEADS, L // q_tile),
        in_specs=[q_spec, kv_spec, v_spec],
        out_specs=q_spec,
        compiler_params=_params("parallel", "parallel", "parallel"),
        name="attention_bounded" if bounded else "attention_online",
    )(q, k, v_in)


def _ffn_kernel(*refs, with_mixer_out):
    if with_mixer_out:
        z_ref, w_mix_ref, g_mix_ref, h_ref, g_pre_ref, w_in_ref, w_out_ref, g_post_ref, out_ref = refs
        m = jnp.dot(z_ref[...], w_mix_ref[...], preferred_element_type=F32)
        h = h_ref[...] + _rms(m, g_mix_ref[...])
    else:
        h_ref, g_pre_ref, w_in_ref, w_out_ref, g_post_ref, out_ref = refs
        h = h_ref[...]
    u = _rms(h, g_pre_ref[...]).astype(BF16)
    d_ff = w_out_ref.shape[0]
    acc = None
    split = (d_ff // MXU_DIM + 1) // 2 * MXU_DIM
    for c0, c1 in ((0, split), (split, d_ff)):
        gate = jnp.dot(u, w_in_ref[:, c0:c1], preferred_element_type=F32)
        up = jnp.dot(u, w_in_ref[:, d_ff + c0:d_ff + c1], preferred_element_type=F32)
        act = (gate * _sigmoid(gate) * up).astype(BF16)
        part = jnp.dot(act, w_out_ref[c0:c1, :], preferred_element_type=F32)
        acc = part if acc is None else acc + part
    out_ref[...] = h + _rms(acc, g_post_ref[...])


def _ffn(h, g_pre, w_in, w_out, g_post, mixer_out=None, drop_leading=0):
    B, L, D = h.shape
    n_out = L - drop_leading
    if drop_leading:
        tile = FINAL_TILE
        flat = lambda a: a.reshape(B * L, a.shape[-1])
        tok = lambda width: pl.BlockSpec((pl.Element(tile), pl.Element(width)),
                                         lambda b, i: (pl.multiple_of(b * L + drop_leading + i * tile, BF16_ROWS), 0))
        out_shape = jax.ShapeDtypeStruct((B * n_out, D), F32)
        out_spec = pl.BlockSpec((tile, D), lambda b, i: (b * (n_out // tile) + i, 0))
    else:
        tile = TOKEN_TILE
        flat = lambda a: a
        tok = lambda width: _tok_spec(tile, width)
        out_shape = jax.ShapeDtypeStruct((B, L, D), F32)
        out_spec = _tok_spec(tile, D)
    args = [flat(h), g_pre, w_in, w_out, g_post]
    in_specs = [tok(D), _const_spec((1, D)), _const_spec(w_in.shape), _const_spec(w_out.shape),
                _const_spec((1, D))]
    if mixer_out is not None:
        z, w_mix, g_mix = mixer_out
        args = [flat(z), w_mix, g_mix] + args
        in_specs = [tok(z.shape[-1]), _const_spec(w_mix.shape), _const_spec((1, D))] + in_specs
    out = pl.pallas_call(
        functools.partial(_ffn_kernel, with_mixer_out=mixer_out is not None),
        out_shape=out_shape,
        grid=(B, n_out // tile),
        in_specs=in_specs,
        out_specs=out_spec,
        compiler_params=_params("parallel", "parallel"),
        name="ffn_after_attention" if mixer_out is not None else "ffn",
    )(*args)
    return out.reshape(B, n_out, D)


def _rope_tables(n_tokens):
    rows = n_tokens // GRID_W
    inv_freq = ROPE_THETA ** (-jnp.arange(0, 2 * ROPE_PAIRS, 2, dtype=F32) / (2 * ROPE_PAIRS))
    ang_r = jnp.arange(rows, dtype=F32)[:, None] * inv_freq
    ang_c = jnp.arange(GRID_W, dtype=F32)[:, None] * inv_freq
    ang_r = jnp.broadcast_to(ang_r[:, None, :], (rows, GRID_W, ROPE_PAIRS)).reshape(n_tokens, ROPE_PAIRS)
    ang_c = jnp.broadcast_to(ang_c[None, :, :], (rows, GRID_W, ROPE_PAIRS)).reshape(n_tokens, ROPE_PAIRS)
    ang = jnp.concatenate([ang_r, ang_c, ang_r, ang_c], axis=-1)
    ang = jnp.concatenate([jnp.zeros((N_META, HEAD_DIM), F32), ang], axis=0)
    sign = jnp.concatenate([-jnp.ones((2 * ROPE_PAIRS,), F32), jnp.ones((2 * ROPE_PAIRS,), F32)])
    return jnp.cos(ang), jnp.sin(ang) * sign


def _head_dim_order():
    p = ROPE_PAIRS
    return jnp.concatenate([jnp.arange(0, p), jnp.arange(2 * p, 3 * p), jnp.arange(p, 2 * p),
                            jnp.arange(3 * p, 4 * p)])


def _reorder_qk_columns(w_qkv):
    order = _head_dim_order()
    n_qk = (N_HEADS + N_KV_HEADS) * HEAD_DIM
    cols = (jnp.arange(n_qk).reshape(-1, HEAD_DIM)[:, :1] + order[None, :]).reshape(-1)
    return jnp.concatenate([w_qkv[:, cols], w_qkv[:, n_qk:]], axis=1)


def kernel(x, meta_tokens, norm_gains, lru_w_in, lru_conv_w, lru_conv_b, lru_gate_w, lru_gate_b,
           lru_lambda, lru_w_out, attn_w_qkv, attn_q_gain, attn_k_gain, attn_w_out, ffn_w_in,
           ffn_w_out):
    B, S, D = x.shape
    depth = norm_gains.shape[0]
    L = N_META + S
    h = (x, meta_tokens.astype(x.dtype))
    cos, sin = _rope_tables(S)
    gains = norm_gains.reshape(depth, 4, 1, D)
    for layer in range(depth):
        g = gains[layer]
        slot = layer // 2
        mixer_out = None
        if layer % 2 == 0:
            y, xb = _lru_in(h, g[0], lru_w_in[slot].astype(BF16), B, L)
            cw = lru_conv_w[slot]
            cb = lru_conv_b[slot].reshape(1, D)
            gw = _gate_weights(lru_gate_w[slot], lru_gate_b[slot])
            lam = lru_lambda[slot].reshape(2, 1, D)
            hf, xs = _lru_fwd(xb, cw, cb, gw[0], lam[0])
            h = _lru_bwd(xs, gw[1], lam[1], hf, y, h, lru_w_out[slot].astype(BF16), g[1])
        else:
            order = _head_dim_order()
            q, k, v = _qkv(h, g[0], _reorder_qk_columns(attn_w_qkv[slot]).astype(BF16),
                           attn_q_gain[slot][order].reshape(1, HEAD_DIM),
                           attn_k_gain[slot][order].reshape(1, HEAD_DIM), cos, sin)
            score_bound = (HEAD_DIM * Q_SCALE * jnp.max(jnp.abs(attn_q_gain[slot]))
                           * jnp.max(jnp.abs(attn_k_gain[slot])))
            o = lax.cond(score_bound <= SCORE_BOUND_MAX,
                         functools.partial(_attention, bounded=True),
                         functools.partial(_attention, bounded=False), q, k, v)
            mixer_out = (o, attn_w_out[slot].astype(BF16), g[1])
        h = _ffn(h, g[2], ffn_w_in[layer].astype(BF16), ffn_w_out[layer].astype(BF16), g[3], mixer_out,
                 drop_leading=N_META if layer == depth - 1 else 0)
    return h
```

```python
import functools

import jax
import jax.numpy as jnp
from jax import lax
from jax.experimental import pallas as pl
from jax.experimental.pallas import tpu as pltpu

F32 = jnp.float32
BF16 = jnp.bfloat16

D_MODEL = 1024
N_META = 16
GRID_W = 64
RMS_EPS = 1e-6
LRU_BLOCKS = 8
LRU_BLOCK_W = 128
LRU_C = 8.0
LOG2_E = 1.4426950408889634
F32_TINY = 1.1754943508222875e-38
HEAD_DIM = 128
N_HEADS = 8
N_KV_HEADS = 2
GQA_GROUP = 4
ROPE_PAIRS = 32
ROPE_THETA = 10000.0

VMEM_LIMIT_BYTES = 56 * 1024 * 1024

TOKEN_TILE = 912
SCAN_TILE = 912
FAST_ATTN_TILES = (2736, 512, 4)
ONLINE_ATTN_TILES = (912, 1024, 2)
FINAL_TILE = 1024
QKV_SUBTILES = 3
MXU_DIM = 256
Q_SCALE = HEAD_DIM ** -0.5 * LOG2_E
SCORE_BOUND_MAX = 40.0
BF16_ROWS = 16
HALO = BF16_ROWS


def _params(*sem):
    return pltpu.CompilerParams(dimension_semantics=sem, vmem_limit_bytes=VMEM_LIMIT_BYTES)


def _tok_spec(tile, width):
    return pl.BlockSpec((None, tile, width), lambda b, i: (b, i, 0))


def _const_spec(shape):
    zeros = (0,) * len(shape)
    return pl.BlockSpec(shape, lambda b, i: zeros, pipeline_mode=pl.Buffered(1))


def _rms(x, g):
    return x * lax.rsqrt(jnp.mean(x * x, axis=-1, keepdims=True) + RMS_EPS) * g


def _sigmoid(x):
    return 0.5 * jnp.tanh(0.5 * x) + 0.5


def _gelu_tanh(x):
    return 0.5 * x * (1.0 + jnp.tanh(0.7978845608028654 * (x + 0.044715 * (x * x * x))))


def _stream_specs(h, tile, chunk_of=lambda i: i):
    if isinstance(h, tuple):
        x, meta = h
        B, S, D = x.shape
        start = lambda b, i: pl.multiple_of(b * S + jnp.maximum(chunk_of(i) * tile - N_META, 0), 8)
        return ([x.reshape(B * S, D), meta],
                [pl.BlockSpec((pl.Element(tile), pl.Element(D)), lambda b, i: (start(b, i), 0)),
                 _const_spec(meta.shape)])
    return [h], [pl.BlockSpec((None, tile, h.shape[-1]), lambda b, i: (b, chunk_of(i), 0))]


def _stream_rows(refs, chunk):
    if len(refs) == 1:
        return refs[0][...]
    x_ref, meta_ref = refs
    xblk = x_ref[...]
    first = jnp.concatenate([meta_ref[...], xblk[:xblk.shape[0] - N_META, :]], axis=0)
    return jnp.where(chunk == 0, first, xblk)


def _lru_in_kernel(*refs):
    *h_refs, g_ref, w_ref, y_ref, xb_ref = refs
    u = _rms(_stream_rows(h_refs, pl.program_id(1)), g_ref[...]).astype(BF16)
    d = y_ref.shape[-1]
    y_ref[...] = _gelu_tanh(jnp.dot(u, w_ref[:, :d], preferred_element_type=F32)).astype(BF16)
    xb_ref[...] = jnp.dot(u, w_ref[:, d:], preferred_element_type=F32).astype(BF16)


def _lru_in(h, g, w, B, L):
    D = w.shape[0]
    out = jax.ShapeDtypeStruct((B, L, D), BF16)
    h_ops, h_specs = _stream_specs(h, TOKEN_TILE)
    return pl.pallas_call(
        _lru_in_kernel,
        out_shape=(out, out),
        grid=(B, L // TOKEN_TILE),
        in_specs=h_specs + [_const_spec((1, D)), _const_spec((D, 2 * D))],
        out_specs=(_tok_spec(TOKEN_TILE, D), _tok_spec(TOKEN_TILE, D)),
        compiler_params=_params("parallel", "parallel"),
        name="lru_in",
    )(*h_ops, g, w)


def _fill_xpad(c, n_chunks, xprev_ref, xcur_ref, xnext_ref, xpad_scr):
    tl = xcur_ref.shape[0]
    w = LRU_BLOCK_W
    prev = jnp.where(c > 0, xprev_ref[HALO - 8:, :].astype(F32), 0.0)
    nxt = jnp.where(c < n_chunks - 1, xnext_ref[:8, :].astype(F32), 0.0)
    for n in range(LRU_BLOCKS):
        sl = slice(n * w, (n + 1) * w)
        xpad_scr[n, pl.ds(0, 8), :] = prev[:, sl]
        xpad_scr[n, pl.ds(8, tl), :] = xcur_ref[:, sl].astype(F32)
        xpad_scr[n, pl.ds(8 + tl, 8), :] = nxt[:, sl]


def _half_log2_decay(lam_ref):
    x = -lam_ref[...]
    e = jnp.exp(-jnp.abs(x))
    u1 = 1.0 + e
    log1p_e = jnp.where(u1 == 1.0, e, jnp.log(u1) * (e / (u1 - 1.0)))
    return (-0.5 * LRU_C * LOG2_E) * (jnp.maximum(x, 0.0) + log1p_e)


def _conv_block(n, xpad_scr, cw_ref, cb_ref):
    w = LRU_BLOCK_W
    sl = slice(n * w, (n + 1) * w)
    seg = (xpad_scr.shape[1] - 16) // 8
    cb = 0.5 * cb_ref[:, sl]
    taps = [0.5 * cw_ref[k:k + 1, sl] for k in range(4)]
    tiles = []
    for r in range(seg):
        t = cb
        for k in range(4):
            t = t + xpad_scr[n, pl.ds(6 + r + k, 8, stride=seg), :] * taps[k]
        tiles.append(t)
    return jnp.concatenate(tiles, axis=0)


def _gate_scan_block(n, reverse, xh, xh_bf16, gw_ref, half_decay, carry_scr, h_scr):
    w = LRU_BLOCK_W
    sl = slice(n * w, (n + 1) * w)
    seg = xh.shape[0] // 8
    ones = (lax.broadcasted_iota(jnp.int32, xh.shape, 1) < 2).astype(BF16)
    gp = jnp.dot(jnp.concatenate([xh_bf16, ones], axis=-1), gw_ref[n],
                 preferred_element_type=F32)
    t_r = jnp.tanh(gp[:, :w])
    t_i = jnp.tanh(gp[:, w:])
    hd = half_decay[:, sl]
    log2_a = t_r * hd + hd
    a = jnp.exp2(log2_a)
    one_minus_a2 = jnp.tanh(log2_a * (-1.0 / LOG2_E)) * (1.0 + a * a)
    root = one_minus_a2 * lax.rsqrt(jnp.maximum(one_minus_a2, F32_TINY))
    b = root * (t_i * xh + xh)

    steps = range(seg - 1, -1, -1) if reverse else range(seg)
    hs = [None] * seg
    ps = [None] * seg
    h_prev = p_prev = None
    for r in steps:
        a_r = a[r * 8:(r + 1) * 8, :]
        b_r = b[r * 8:(r + 1) * 8, :]
        hs[r] = b_r if h_prev is None else a_r * h_prev + b_r
        ps[r] = a_r if p_prev is None else a_r * p_prev
        h_prev, p_prev = hs[r], ps[r]
    carry = carry_scr[n:n + 1, :]
    seg_in = [None] * 8
    for j in (range(7, -1, -1) if reverse else range(8)):
        seg_in[j] = carry
        carry = h_prev[j:j + 1, :] + p_prev[j:j + 1, :] * carry
    carry_scr[n:n + 1, :] = carry
    seg_in = jnp.concatenate(seg_in, axis=0)
    for r in range(seg):
        h_scr[n, pl.ds(r, 8, stride=seg), :] = hs[r] + ps[r] * seg_in


def _lru_fwd_kernel(xprev_ref, xcur_ref, xnext_ref, cw_ref, cb_ref, gw_ref, lam_ref,
                    hf_ref, xs_ref, xpad_scr, h_scr, carry_scr):
    c = pl.program_id(1)
    n_chunks = pl.num_programs(1)

    @pl.when(c == 0)
    def _():
        carry_scr[...] = jnp.zeros_like(carry_scr)

    _fill_xpad(c, n_chunks, xprev_ref, xcur_ref, xnext_ref, xpad_scr)
    half_decay = _half_log2_decay(lam_ref)
    w = LRU_BLOCK_W
    for n in range(LRU_BLOCKS):
        sl = slice(n * w, (n + 1) * w)
        xh = _conv_block(n, xpad_scr, cw_ref, cb_ref)
        xh_bf16 = xh.astype(BF16)
        xs_ref[:, sl] = xh_bf16
        _gate_scan_block(n, False, xh, xh_bf16, gw_ref, half_decay, carry_scr, h_scr)
        hf_ref[:, sl] = h_scr[n].astype(BF16)


def _lru_bwd_kernel(xs_ref, gw_ref, lam_ref, hf_ref, y_ref, wout_ref, g_ref, *rest):
    *h_refs, out_ref, h_scr, carry_scr, z_scr = rest
    i = pl.program_id(1)
    n_chunks = pl.num_programs(1)
    c = n_chunks - 1 - i

    @pl.when(i == 0)
    def _():
        carry_scr[...] = jnp.zeros_like(carry_scr)

    half_decay = _half_log2_decay(lam_ref)
    w = LRU_BLOCK_W
    for n in range(LRU_BLOCKS):
        sl = slice(n * w, (n + 1) * w)
        xh_bf16 = xs_ref[:, sl]
        _gate_scan_block(n, True, xh_bf16.astype(F32), xh_bf16, gw_ref, half_decay, carry_scr, h_scr)
        rec = hf_ref[:, sl].astype(F32) + h_scr[n]
        z_scr[:, sl] = (rec * y_ref[:, sl].astype(F32)).astype(BF16)
    m = jnp.dot(z_scr[...], wout_ref[...], preferred_element_type=F32)
    out_ref[...] = _stream_rows(h_refs, c) + _rms(m, g_ref[...])


def _gate_spec():
    return _const_spec((LRU_BLOCKS, 2 * LRU_BLOCK_W, 2 * LRU_BLOCK_W))


def _lru_fwd(xb, cw, cb, gw, lam):
    B, L, D = xb.shape
    tl = SCAN_TILE
    per = tl // HALO
    last = L // HALO - 1
    cur = _tok_spec(tl, D)
    prev = pl.BlockSpec((None, HALO, D), lambda b, i: (b, jnp.maximum(i * per - 1, 0), 0))
    nxt = pl.BlockSpec((None, HALO, D), lambda b, i: (b, jnp.minimum((i + 1) * per, last), 0))
    out = jax.ShapeDtypeStruct((B, L, D), BF16)
    return pl.pallas_call(
        _lru_fwd_kernel,
        out_shape=(out, out),
        grid=(B, L // tl),
        in_specs=[prev, cur, nxt, _const_spec((4, D)), _const_spec((1, D)), _gate_spec(), _const_spec((1, D))],
        out_specs=(cur, cur),
        scratch_shapes=[pltpu.VMEM((LRU_BLOCKS, tl + 16, LRU_BLOCK_W), F32),
                        pltpu.VMEM((LRU_BLOCKS, tl, LRU_BLOCK_W), F32),
                        pltpu.VMEM((LRU_BLOCKS, LRU_BLOCK_W), F32)],
        compiler_params=_params("parallel", "arbitrary"),
        name="lru_fwd",
    )(xb, xb, xb, cw, cb, gw, lam)


def _lru_bwd(xs, gw, lam, hf, y, h, w_out, g):
    B, L, D = xs.shape
    tl = SCAN_TILE
    n_chunks = L // tl
    chunk_of = lambda i: n_chunks - 1 - i
    cur = pl.BlockSpec((None, tl, D), lambda b, i: (b, chunk_of(i), 0))
    h_ops, h_specs = _stream_specs(h, tl, chunk_of)
    return pl.pallas_call(
        _lru_bwd_kernel,
        out_shape=jax.ShapeDtypeStruct((B, L, D), F32),
        grid=(B, n_chunks),
        in_specs=[cur, _gate_spec(), _const_spec((1, D)), cur, cur, _const_spec((D, D)),
                  _const_spec((1, D))] + h_specs,
        out_specs=cur,
        scratch_shapes=[pltpu.VMEM((LRU_BLOCKS, tl, LRU_BLOCK_W), F32),
                        pltpu.VMEM((LRU_BLOCKS, LRU_BLOCK_W), F32),
                        pltpu.VMEM((tl, D), BF16)],
        compiler_params=_params("parallel", "arbitrary"),
        name="lru_bwd",
    )(xs, gw, lam, hf, y, w_out, g, *h_ops)


def _gate_weights(gate_w, gate_b):
    w = LRU_BLOCK_W
    wts = jnp.concatenate([gate_w[:, 0], gate_w[:, 1]], axis=-1)
    bias = 0.5 * gate_b.reshape(2, 2, LRU_BLOCKS, w).transpose(0, 2, 1, 3).reshape(2, LRU_BLOCKS, 1, 2 * w)
    hi = bias.astype(BF16)
    lo = (bias - hi.astype(F32)).astype(BF16)
    pad = jnp.zeros((2, LRU_BLOCKS, w - 2, 2 * w), BF16)
    return jnp.concatenate([wts.astype(BF16), hi, lo, pad], axis=2)


def _rope(x, cos, sin_signed):
    return x * cos + pltpu.roll(x, HEAD_DIM // 2, 1) * sin_signed


def _qkv_kernel(h_ref, g_ref, w_ref, qg_ref, kg_ref, cos_ref, sin_ref, q_ref, k_ref, v_ref):
    hd = HEAD_DIM
    nq = N_HEADS * hd
    nk = N_KV_HEADS * hd
    q_gain = qg_ref[...] * Q_SCALE
    sub = h_ref.shape[0] // QKV_SUBTILES
    for s in range(QKV_SUBTILES):
        rows = pl.ds(s * sub, sub)
        u = _rms(h_ref[rows, :], g_ref[...]).astype(BF16)
        cos = cos_ref[rows, :]
        sin = sin_ref[rows, :]
        q = jnp.dot(u, w_ref[:, :nq], preferred_element_type=F32)
        for hh in range(N_HEADS):
            sl = slice(hh * hd, (hh + 1) * hd)
            q_ref[rows, sl] = _rope(_rms(q[:, sl], q_gain), cos, sin).astype(BF16)
        kv = jnp.dot(u, w_ref[:, nq:], preferred_element_type=F32)
        for hh in range(N_KV_HEADS):
            sl = slice(hh * hd, (hh + 1) * hd)
            k_ref[rows, sl] = _rope(_rms(kv[:, sl], kg_ref[...]), cos, sin).astype(BF16)
        v_ref[rows, :] = kv[:, nk:].astype(BF16)


def _qkv(h, g, w, qg, kg, cos, sin):
    B, L, D = h.shape
    nq = N_HEADS * HEAD_DIM
    nk = N_KV_HEADS * HEAD_DIM
    tab = pl.BlockSpec((TOKEN_TILE, HEAD_DIM), lambda b, i: (i, 0))
    return pl.pallas_call(
        _qkv_kernel,
        out_shape=(jax.ShapeDtypeStruct((B, L, nq), BF16), jax.ShapeDtypeStruct((B, L, nk), BF16),
                   jax.ShapeDtypeStruct((B, L, nk), BF16)),
        grid=(B, L // TOKEN_TILE),
        in_specs=[_tok_spec(TOKEN_TILE, D), _const_spec((1, D)), _const_spec((D, nq + 2 * nk)),
                  _const_spec((1, HEAD_DIM)), _const_spec((1, HEAD_DIM)), tab, tab],
        out_specs=(_tok_spec(TOKEN_TILE, nq), _tok_spec(TOKEN_TILE, nk), _tok_spec(TOKEN_TILE, nk)),
        compiler_params=_params("parallel", "parallel"),
        name="qkv",
    )(h, g, w, qg, kg, cos, sin)


def _scores(q, kc):
    return lax.dot_general(q, kc, (((1,), (1,)), ((), ())), preferred_element_type=F32)


def _attn_bounded_kernel(q_ref, k_ref, vt_ref, o_ref, *, key_chunk, unroll):
    tq = q_ref.shape[0]
    L = k_ref.shape[0]
    hd = HEAD_DIM
    qs = [q_ref[:, g * hd:(g + 1) * hd] for g in range(GQA_GROUP)]

    def step(kc, vtc, carry):
        out = []
        for g in range(GQA_GROUP):
            p = jnp.exp2(_scores(kc, qs[g]))
            den = p.reshape(-1, 8, tq).sum(axis=0)
            acc = jnp.dot(vtc, p.astype(BF16), preferred_element_type=F32)
            if carry is not None:
                acc, den = carry[g][0] + acc, carry[g][1] + den
            out.append((acc, den))
        return tuple(out)

    n_full = L // key_chunk
    tail = L - n_full * key_chunk
    if tail:
        carry = step(k_ref[pl.ds(n_full * key_chunk, tail), :], vt_ref[:, pl.ds(n_full * key_chunk, tail)], None)
    else:
        carry = tuple((jnp.zeros((hd, tq), F32), jnp.zeros((8, tq), F32)) for _ in range(GQA_GROUP))

    def body(j, carry):
        k0 = pl.multiple_of(j * key_chunk, key_chunk)
        return step(k_ref[pl.ds(k0, key_chunk), :], vt_ref[:, pl.ds(k0, key_chunk)], carry)

    carry = lax.fori_loop(0, n_full, body, carry, unroll=unroll)
    for g in range(GQA_GROUP):
        acc, den = carry[g]
        out_t = acc * (1.0 / jnp.sum(den, axis=0, keepdims=True))
        o_ref[:, g * hd:(g + 1) * hd] = out_t.T.astype(BF16)


def _attn_online_kernel(q_ref, k_ref, v_ref, o_ref, *, key_chunk, unroll):
    tq = q_ref.shape[0]
    L = k_ref.shape[0]
    hd = HEAD_DIM
    qs = [q_ref[:, g * hd:(g + 1) * hd] for g in range(GQA_GROUP)]

    def step(kc, vc, carry):
        out = []
        for g in range(GQA_GROUP):
            m, l, acc = carry[g]
            s = _scores(qs[g], kc)
            m_new = jnp.maximum(m, jnp.max(s, axis=-1, keepdims=True))
            alpha = jnp.exp2(m - m_new)
            p = jnp.exp2(s - m_new)
            l = alpha * l + jnp.sum(p, axis=-1, keepdims=True)
            acc = alpha * acc + jnp.dot(p.astype(BF16), vc, preferred_element_type=F32)
            out.append((m_new, l, acc))
        return tuple(out)

    n_full = L // key_chunk
    tail = L - n_full * key_chunk
    carry = tuple((jnp.full((tq, 1), -jnp.inf, F32), jnp.zeros((tq, 1), F32), jnp.zeros((tq, hd), F32))
                  for _ in range(GQA_GROUP))
    if tail:
        carry = step(k_ref[pl.ds(n_full * key_chunk, tail), :], v_ref[pl.ds(n_full * key_chunk, tail), :],
                     carry)

    def body(j, carry):
        k0 = pl.multiple_of(j * key_chunk, key_chunk)
        return step(k_ref[pl.ds(k0, key_chunk), :], v_ref[pl.ds(k0, key_chunk), :], carry)

    carry = lax.fori_loop(0, n_full, body, carry, unroll=unroll)
    for g in range(GQA_GROUP):
        _, l, acc = carry[g]
        o_ref[:, g * hd:(g + 1) * hd] = (acc * (1.0 / l)).astype(BF16)


def _attention(q, k, v, *, bounded):
    B, L, nq = q.shape
    gw = GQA_GROUP * HEAD_DIM
    q_tile, key_chunk, unroll = FAST_ATTN_TILES if bounded else ONLINE_ATTN_TILES
    q_spec = pl.BlockSpec((None, q_tile, gw), lambda b, kh, i: (b, i, kh))
    kv_spec = pl.BlockSpec((None, L, HEAD_DIM), lambda b, kh, i: (b, 0, kh))
    if bounded:
        v_in = v.reshape(B, L, N_KV_HEADS, HEAD_DIM).transpose(0, 2, 3, 1)
        v_spec = pl.BlockSpec((None, None, HEAD_DIM, L), lambda b, kh, i: (b, kh, 0, 0))
    else:
        v_in, v_spec = v, kv_spec
    return pl.pallas_call(
        functools.partial(_attn_bounded_kernel if bounded else _attn_online_kernel, key_chunk=key_chunk,
                          unroll=unroll),
        out_shape=jax.ShapeDtypeStruct((B, L, nq), BF16),
        grid=(B, N_KV_HEADS, L // q_tile),
        in_specs=[q_spec, kv_spec, v_spec],
        out_specs=q_spec,
        compiler_params=_params("parallel", "parallel", "parallel"),
        name="attention_bounded" if bounded else "attention_online",
    )(q, k, v_in)


def _ffn_kernel(*refs, with_mixer_out):
    if with_mixer_out:
        z_ref, w_mix_ref, g_mix_ref, h_ref, g_pre_ref, w_in_ref, w_out_ref, g_post_ref, out_ref = refs
        m = jnp.dot(z_ref[...], w_mix_ref[...], preferred_element_type=F32)
        h = h_ref[...] + _rms(m, g_mix_ref[...])
    else:
        h_ref, g_pre_ref, w_in_ref, w_out_ref, g_post_ref, out_ref = refs
        h = h_ref[...]
    u = _rms(h, g_pre_ref[...]).astype(BF16)
    d_ff = w_out_ref.shape[0]
    acc = None
    split = (d_ff // MXU_DIM + 1) // 2 * MXU_DIM
    for c0, c1 in ((0, split), (split, d_ff)):
        gate = jnp.dot(u, w_in_ref[:, c0:c1], preferred_element_type=F32)
        up = jnp.dot(u, w_in_ref[:, d_ff + c0:d_ff + c1], preferred_element_type=F32)
        act = (gate * _sigmoid(gate) * up).astype(BF16)
        part = jnp.dot(act, w_out_ref[c0:c1, :], preferred_element_type=F32)
        acc = part if acc is None else acc + part
    out_ref[...] = h + _rms(acc, g_post_ref[...])


def _ffn(h, g_pre, w_in, w_out, g_post, mixer_out=None, drop_leading=0):
    B, L, D = h.shape
    n_out = L - drop_leading
    if drop_leading:
        tile = FINAL_TILE
        flat = lambda a: a.reshape(B * L, a.shape[-1])
        tok = lambda width: pl.BlockSpec((pl.Element(tile), pl.Element(width)),
                                         lambda b, i: (pl.multiple_of(b * L + drop_leading + i * tile, BF16_ROWS), 0))
        out_shape = jax.ShapeDtypeStruct((B * n_out, D), F32)
        out_spec = pl.BlockSpec((tile, D), lambda b, i: (b * (n_out // tile) + i, 0))
    else:
        tile = TOKEN_TILE
        flat = lambda a: a
        tok = lambda width: _tok_spec(tile, width)
        out_shape = jax.ShapeDtypeStruct((B, L, D), F32)
        out_spec = _tok_spec(tile, D)
    args = [flat(h), g_pre, w_in, w_out, g_post]
    in_specs = [tok(D), _const_spec((1, D)), _const_spec(w_in.shape), _const_spec(w_out.shape),
                _const_spec((1, D))]
    if mixer_out is not None:
        z, w_mix, g_mix = mixer_out
        args = [flat(z), w_mix, g_mix] + args
        in_specs = [tok(z.shape[-1]), _const_spec(w_mix.shape), _const_spec((1, D))] + in_specs
    out = pl.pallas_call(
        functools.partial(_ffn_kernel, with_mixer_out=mixer_out is not None),
        out_shape=out_shape,
        grid=(B, n_out // tile),
        in_specs=in_specs,
        out_specs=out_spec,
        compiler_params=_params("parallel", "parallel"),
        name="ffn_after_attention" if mixer_out is not None else "ffn",
    )(*args)
    return out.reshape(B, n_out, D)


def _rope_tables(n_tokens):
    rows = n_tokens // GRID_W
    inv_freq = ROPE_THETA ** (-jnp.arange(0, 2 * ROPE_PAIRS, 2, dtype=F32) / (2 * ROPE_PAIRS))
    ang_r = jnp.arange(rows, dtype=F32)[:, None] * inv_freq
    ang_c = jnp.arange(GRID_W, dtype=F32)[:, None] * inv_freq
    ang_r = jnp.broadcast_to(ang_r[:, None, :], (rows, GRID_W, ROPE_PAIRS)).reshape(n_tokens, ROPE_PAIRS)
    ang_c = jnp.broadcast_to(ang_c[None, :, :], (rows, GRID_W, ROPE_PAIRS)).reshape(n_tokens, ROPE_PAIRS)
    ang = jnp.concatenate([ang_r, ang_c, ang_r, ang_c], axis=-1)
    ang = jnp.concatenate([jnp.zeros((N_META, HEAD_DIM), F32), ang], axis=0)
    sign = jnp.concatenate([-jnp.ones((2 * ROPE_PAIRS,), F32), jnp.ones((2 * ROPE_PAIRS,), F32)])
    return jnp.cos(ang), jnp.sin(ang) * sign


def _head_dim_order():
    p = ROPE_PAIRS
    return jnp.concatenate([jnp.arange(0, p), jnp.arange(2 * p, 3 * p), jnp.arange(p, 2 * p),
                            jnp.arange(3 * p, 4 * p)])


def _reorder_qk_columns(w_qkv):
    order = _head_dim_order()
    n_qk = (N_HEADS + N_KV_HEADS) * HEAD_DIM
    cols = (jnp.arange(n_qk).reshape(-1, HEAD_DIM)[:, :1] + order[None, :]).reshape(-1)
    return jnp.concatenate([w_qkv[:, cols], w_qkv[:, n_qk:]], axis=1)


def kernel(x, meta_tokens, norm_gains, lru_w_in, lru_conv_w, lru_conv_b, lru_gate_w, lru_gate_b,
           lru_lambda, lru_w_out, attn_w_qkv, attn_q_gain, attn_k_gain, attn_w_out, ffn_w_in,
           ffn_w_out):
    B, S, D = x.shape
    depth = norm_gains.shape[0]
    L = N_META + S
    h = (x, meta_tokens.astype(x.dtype))
    cos, sin = _rope_tables(S)
    gains = norm_gains.reshape(depth, 4, 1, D)
    for layer in range(depth):
        g = gains[layer]
        slot = layer // 2
        mixer_out = None
        if layer % 2 == 0:
            y, xb = _lru_in(h, g[0], lru_w_in[slot].astype(BF16), B, L)
            cw = lru_conv_w[slot]
            cb = lru_conv_b[slot].reshape(1, D)
            gw = _gate_weights(lru_gate_w[slot], lru_gate_b[slot])
            lam = lru_lambda[slot].reshape(2, 1, D)
            hf, xs = _lru_fwd(xb, cw, cb, gw[0], lam[0])
            h = _lru_bwd(xs, gw[1], lam[1], hf, y, h, lru_w_out[slot].astype(BF16), g[1])
        else:
            order = _head_dim_order()
            q, k, v = _qkv(h, g[0], _reorder_qk_columns(attn_w_qkv[slot]).astype(BF16),
                           attn_q_gain[slot][order].reshape(1, HEAD_DIM),
                           attn_k_gain[slot][order].reshape(1, HEAD_DIM), cos, sin)
            score_bound = (HEAD_DIM * Q_SCALE * jnp.max(jnp.abs(attn_q_gain[slot]))
                           * jnp.max(jnp.abs(attn_k_gain[slot])))
            o = lax.cond(score_bound <= SCORE_BOUND_MAX,
                         functools.partial(_attention, bounded=True),
                         functools.partial(_attention, bounded=False), q, k, v)
            mixer_out = (o, attn_w_out[slot].astype(BF16), g[1])
        h = _ffn(h, g[2], ffn_w_in[layer].astype(BF16), ffn_w_out[layer].astype(BF16), g[3], mixer_out,
                 drop_leading=N_META if layer == depth - 1 else 0)
    return h
```

```python
import functools

import jax
import jax.numpy as jnp
from jax import lax
from jax.experimental import pallas as pl
from jax.experimental.pallas import tpu as pltpu

F32 = jnp.float32
BF16 = jnp.bfloat16

D_MODEL = 1024
N_META = 16
GRID_W = 64
RMS_EPS = 1e-6
LRU_BLOCKS = 8
LRU_BLOCK_W = 128
LRU_C = 8.0
LOG2_E = 1.4426950408889634
F32_TINY = 1.1754943508222875e-38
HEAD_DIM = 128
N_HEADS = 8
N_KV_HEADS = 2
GQA_GROUP = 4
ROPE_PAIRS = 32
ROPE_THETA = 10000.0

VMEM_LIMIT_BYTES = 56 * 1024 * 1024

TOKEN_TILE = 912
SCAN_TILE = 912
FAST_ATTN_TILES = (2736, 512, 4)
ONLINE_ATTN_TILES = (912, 1024, 2)
FINAL_TILE = 1024
QKV_SUBTILES = 3
MXU_DIM = 256
FFN_CHUNK_TILES = 4
Q_SCALE = HEAD_DIM ** -0.5 * LOG2_E
SCORE_BOUND_MAX = 40.0
BF16_ROWS = 16
HALO = BF16_ROWS


def _params(*sem):
    return pltpu.CompilerParams(dimension_semantics=sem, vmem_limit_bytes=VMEM_LIMIT_BYTES)


def _tok_spec(tile, width):
    return pl.BlockSpec((None, tile, width), lambda b, i: (b, i, 0))


def _const_spec(shape):
    zeros = (0,) * len(shape)
    return pl.BlockSpec(shape, lambda b, i: zeros, pipeline_mode=pl.Buffered(1))


def _rms(x, g):
    return x * lax.rsqrt(jnp.mean(x * x, axis=-1, keepdims=True) + RMS_EPS) * g


def _sigmoid(x):
    return 0.5 * jnp.tanh(0.5 * x) + 0.5


def _gelu_tanh(x):
    return 0.5 * x * (1.0 + jnp.tanh(0.7978845608028654 * (x + 0.044715 * (x * x * x))))


def _stream_specs(h, tile, chunk_of=lambda i: i):
    if isinstance(h, tuple):
        x, meta = h
        B, S, D = x.shape
        start = lambda b, i: pl.multiple_of(b * S + jnp.maximum(chunk_of(i) * tile - N_META, 0), 8)
        return ([x.reshape(B * S, D), meta],
                [pl.BlockSpec((pl.Element(tile), pl.Element(D)), lambda b, i: (start(b, i), 0)),
                 _const_spec(meta.shape)])
    return [h], [pl.BlockSpec((None, tile, h.shape[-1]), lambda b, i: (b, chunk_of(i), 0))]


def _stream_rows(refs, chunk):
    if len(refs) == 1:
        return refs[0][...]
    x_ref, meta_ref = refs
    xblk = x_ref[...]
    first = jnp.concatenate([meta_ref[...], xblk[:xblk.shape[0] - N_META, :]], axis=0)
    return jnp.where(chunk == 0, first, xblk)


def _lru_in_kernel(*refs):
    *h_refs, g_ref, w_ref, y_ref, xb_ref = refs
    u = _rms(_stream_rows(h_refs, pl.program_id(1)), g_ref[...]).astype(BF16)
    d = y_ref.shape[-1]
    y_ref[...] = _gelu_tanh(jnp.dot(u, w_ref[:, :d], preferred_element_type=F32)).astype(BF16)
    xb_ref[...] = jnp.dot(u, w_ref[:, d:], preferred_element_type=F32).astype(BF16)


def _lru_in(h, g, w, B, L):
    D = w.shape[0]
    out = jax.ShapeDtypeStruct((B, L, D), BF16)
    h_ops, h_specs = _stream_specs(h, TOKEN_TILE)
    return pl.pallas_call(
        _lru_in_kernel,
        out_shape=(out, out),
        grid=(B, L // TOKEN_TILE),
        in_specs=h_specs + [_const_spec((1, D)), _const_spec((D, 2 * D))],
        out_specs=(_tok_spec(TOKEN_TILE, D), _tok_spec(TOKEN_TILE, D)),
        compiler_params=_params("parallel", "parallel"),
        name="lru_in",
    )(*h_ops, g, w)


def _fill_xpad(c, n_chunks, xprev_ref, xcur_ref, xnext_ref, xpad_scr):
    tl = xcur_ref.shape[0]
    w = LRU_BLOCK_W
    prev = jnp.where(c > 0, xprev_ref[HALO - 8:, :].astype(F32), 0.0)
    nxt = jnp.where(c < n_chunks - 1, xnext_ref[:8, :].astype(F32), 0.0)
    for n in range(LRU_BLOCKS):
        sl = slice(n * w, (n + 1) * w)
        xpad_scr[n, pl.ds(0, 8), :] = prev[:, sl]
        xpad_scr[n, pl.ds(8, tl), :] = xcur_ref[:, sl].astype(F32)
        xpad_scr[n, pl.ds(8 + tl, 8), :] = nxt[:, sl]


def _half_log2_decay(lam_ref):
    x = -lam_ref[...]
    e = jnp.exp(-jnp.abs(x))
    u1 = 1.0 + e
    log1p_e = jnp.where(u1 == 1.0, e, jnp.log(u1) * (e / (u1 - 1.0)))
    return (-0.5 * LRU_C * LOG2_E) * (jnp.maximum(x, 0.0) + log1p_e)


def _conv_block(n, xpad_scr, cw_ref, cb_ref):
    w = LRU_BLOCK_W
    sl = slice(n * w, (n + 1) * w)
    seg = (xpad_scr.shape[1] - 16) // 8
    cb = 0.5 * cb_ref[:, sl]
    taps = [0.5 * cw_ref[k:k + 1, sl] for k in range(4)]
    tiles = []
    for r in range(seg):
        t = cb
        for k in range(4):
            t = t + xpad_scr[n, pl.ds(6 + r + k, 8, stride=seg), :] * taps[k]
        tiles.append(t)
    return jnp.concatenate(tiles, axis=0)


def _gate_scan_block(n, reverse, xh, xh_bf16, gw_ref, half_decay, carry_scr, h_scr):
    w = LRU_BLOCK_W
    sl = slice(n * w, (n + 1) * w)
    seg = xh.shape[0] // 8
    ones = (lax.broadcasted_iota(jnp.int32, xh.shape, 1) < 2).astype(BF16)
    gp = jnp.dot(jnp.concatenate([xh_bf16, ones], axis=-1), gw_ref[n],
                 preferred_element_type=F32)
    t_r = jnp.tanh(gp[:, :w])
    t_i = jnp.tanh(gp[:, w:])
    hd = half_decay[:, sl]
    log2_a = t_r * hd + hd
    a = jnp.exp2(log2_a)
    one_minus_a2 = jnp.tanh(log2_a * (-1.0 / LOG2_E)) * (1.0 + a * a)
    root = one_minus_a2 * lax.rsqrt(jnp.maximum(one_minus_a2, F32_TINY))
    b = root * (t_i * xh + xh)

    steps = range(seg - 1, -1, -1) if reverse else range(seg)
    hs = [None] * seg
    ps = [None] * seg
    h_prev = p_prev = None
    for r in steps:
        a_r = a[r * 8:(r + 1) * 8, :]
        b_r = b[r * 8:(r + 1) * 8, :]
        hs[r] = b_r if h_prev is None else a_r * h_prev + b_r
        ps[r] = a_r if p_prev is None else a_r * p_prev
        h_prev, p_prev = hs[r], ps[r]
    carry = carry_scr[n:n + 1, :]
    seg_in = [None] * 8
    for j in (range(7, -1, -1) if reverse else range(8)):
        seg_in[j] = carry
        carry = h_prev[j:j + 1, :] + p_prev[j:j + 1, :] * carry
    carry_scr[n:n + 1, :] = carry
    seg_in = jnp.concatenate(seg_in, axis=0)
    for r in range(seg):
        h_scr[n, pl.ds(r, 8, stride=seg), :] = hs[r] + ps[r] * seg_in


def _lru_fwd_kernel(xprev_ref, xcur_ref, xnext_ref, cw_ref, cb_ref, gw_ref, lam_ref,
                    hf_ref, xs_ref, xpad_scr, h_scr, carry_scr):
    c = pl.program_id(1)
    n_chunks = pl.num_programs(1)

    @pl.when(c == 0)
    def _():
        carry_scr[...] = jnp.zeros_like(carry_scr)

    _fill_xpad(c, n_chunks, xprev_ref, xcur_ref, xnext_ref, xpad_scr)
    half_decay = _half_log2_decay(lam_ref)
    w = LRU_BLOCK_W
    for n in range(LRU_BLOCKS):
        sl = slice(n * w, (n + 1) * w)
        xh = _conv_block(n, xpad_scr, cw_ref, cb_ref)
        xh_bf16 = xh.astype(BF16)
        xs_ref[:, sl] = xh_bf16
        _gate_scan_block(n, False, xh, xh_bf16, gw_ref, half_decay, carry_scr, h_scr)
        hf_ref[:, sl] = h_scr[n].astype(BF16)


def _lru_bwd_kernel(xs_ref, gw_ref, lam_ref, hf_ref, y_ref, wout_ref, g_ref, *rest):
    *h_refs, out_ref, h_scr, carry_scr, z_scr = rest
    i = pl.program_id(1)
    n_chunks = pl.num_programs(1)
    c = n_chunks - 1 - i

    @pl.when(i == 0)
    def _():
        carry_scr[...] = jnp.zeros_like(carry_scr)

    half_decay = _half_log2_decay(lam_ref)
    w = LRU_BLOCK_W
    for n in range(LRU_BLOCKS):
        sl = slice(n * w, (n + 1) * w)
        xh_bf16 = xs_ref[:, sl]
        _gate_scan_block(n, True, xh_bf16.astype(F32), xh_bf16, gw_ref, half_decay, carry_scr, h_scr)
        rec = hf_ref[:, sl].astype(F32) + h_scr[n]
        z_scr[:, sl] = (rec * y_ref[:, sl].astype(F32)).astype(BF16)
    m = jnp.dot(z_scr[...], wout_ref[...], preferred_element_type=F32)
    out_ref[...] = _stream_rows(h_refs, c) + _rms(m, g_ref[...])


def _gate_spec():
    return _const_spec((LRU_BLOCKS, 2 * LRU_BLOCK_W, 2 * LRU_BLOCK_W))


def _lru_fwd(xb, cw, cb, gw, lam):
    B, L, D = xb.shape
    tl = SCAN_TILE
    per = tl // HALO
    last = L // HALO - 1
    cur = _tok_spec(tl, D)
    prev = pl.BlockSpec((None, HALO, D), lambda b, i: (b, jnp.maximum(i * per - 1, 0), 0))
    nxt = pl.BlockSpec((None, HALO, D), lambda b, i: (b, jnp.minimum((i + 1) * per, last), 0))
    out = jax.ShapeDtypeStruct((B, L, D), BF16)
    return pl.pallas_call(
        _lru_fwd_kernel,
        out_shape=(out, out),
        grid=(B, L // tl),
        in_specs=[prev, cur, nxt, _const_spec((4, D)), _const_spec((1, D)), _gate_spec(), _const_spec((1, D))],
        out_specs=(cur, cur),
        scratch_shapes=[pltpu.VMEM((LRU_BLOCKS, tl + 16, LRU_BLOCK_W), F32),
                        pltpu.VMEM((LRU_BLOCKS, tl, LRU_BLOCK_W), F32),
                        pltpu.VMEM((LRU_BLOCKS, LRU_BLOCK_W), F32)],
        compiler_params=_params("parallel", "arbitrary"),
        name="lru_fwd",
    )(xb, xb, xb, cw, cb, gw, lam)


def _lru_bwd(xs, gw, lam, hf, y, h, w_out, g):
    B, L, D = xs.shape
    tl = SCAN_TILE
    n_chunks = L // tl
    chunk_of = lambda i: n_chunks - 1 - i
    cur = pl.BlockSpec((None, tl, D), lambda b, i: (b, chunk_of(i), 0))
    h_ops, h_specs = _stream_specs(h, tl, chunk_of)
    return pl.pallas_call(
        _lru_bwd_kernel,
        out_shape=jax.ShapeDtypeStruct((B, L, D), F32),
        grid=(B, n_chunks),
        in_specs=[cur, _gate_spec(), _const_spec((1, D)), cur, cur, _const_spec((D, D)),
                  _const_spec((1, D))] + h_specs,
        out_specs=cur,
        scratch_shapes=[pltpu.VMEM((LRU_BLOCKS, tl, LRU_BLOCK_W), F32),
                        pltpu.VMEM((LRU_BLOCKS, LRU_BLOCK_W), F32),
                        pltpu.VMEM((tl, D), BF16)],
        compiler_params=_params("parallel", "arbitrary"),
        name="lru_bwd",
    )(xs, gw, lam, hf, y, w_out, g, *h_ops)


def _gate_weights(gate_w, gate_b):
    w = LRU_BLOCK_W
    wts = jnp.concatenate([gate_w[:, 0], gate_w[:, 1]], axis=-1)
    bias = 0.5 * gate_b.reshape(2, 2, LRU_BLOCKS, w).transpose(0, 2, 1, 3).reshape(2, LRU_BLOCKS, 1, 2 * w)
    hi = bias.astype(BF16)
    lo = (bias - hi.astype(F32)).astype(BF16)
    pad = jnp.zeros((2, LRU_BLOCKS, w - 2, 2 * w), BF16)
    return jnp.concatenate([wts.astype(BF16), hi, lo, pad], axis=2)


def _rope(x, cos, sin_signed):
    return x * cos + pltpu.roll(x, HEAD_DIM // 2, 1) * sin_signed


def _qkv_kernel(h_ref, g_ref, w_ref, qg_ref, kg_ref, cos_ref, sin_ref, q_ref, k_ref, v_ref):
    hd = HEAD_DIM
    nq = N_HEADS * hd
    nk = N_KV_HEADS * hd
    q_gain = qg_ref[...] * Q_SCALE
    sub = h_ref.shape[0] // QKV_SUBTILES
    for s in range(QKV_SUBTILES):
        rows = pl.ds(s * sub, sub)
        u = _rms(h_ref[rows, :], g_ref[...]).astype(BF16)
        cos = cos_ref[rows, :]
        sin = sin_ref[rows, :]
        q = jnp.dot(u, w_ref[:, :nq], preferred_element_type=F32)
        for hh in range(N_HEADS):
            sl = slice(hh * hd, (hh + 1) * hd)
            q_ref[rows, sl] = _rope(_rms(q[:, sl], q_gain), cos, sin).astype(BF16)
        kv = jnp.dot(u, w_ref[:, nq:], preferred_element_type=F32)
        for hh in range(N_KV_HEADS):
            sl = slice(hh * hd, (hh + 1) * hd)
            k_ref[rows, sl] = _rope(_rms(kv[:, sl], kg_ref[...]), cos, sin).astype(BF16)
        v_ref[rows, :] = kv[:, nk:].astype(BF16)


def _qkv(h, g, w, qg, kg, cos, sin):
    B, L, D = h.shape
    nq = N_HEADS * HEAD_DIM
    nk = N_KV_HEADS * HEAD_DIM
    tab = pl.BlockSpec((TOKEN_TILE, HEAD_DIM), lambda b, i: (i, 0))
    return pl.pallas_call(
        _qkv_kernel,
        out_shape=(jax.ShapeDtypeStruct((B, L, nq), BF16), jax.ShapeDtypeStruct((B, L, nk), BF16),
                   jax.ShapeDtypeStruct((B, L, nk), BF16)),
        grid=(B, L // TOKEN_TILE),
        in_specs=[_tok_spec(TOKEN_TILE, D), _const_spec((1, D)), _const_spec((D, nq + 2 * nk)),
                  _const_spec((1, HEAD_DIM)), _const_spec((1, HEAD_DIM)), tab, tab],
        out_specs=(_tok_spec(TOKEN_TILE, nq), _tok_spec(TOKEN_TILE, nk), _tok_spec(TOKEN_TILE, nk)),
        compiler_params=_params("parallel", "parallel"),
        name="qkv",
    )(h, g, w, qg, kg, cos, sin)


def _scores(q, kc):
    return lax.dot_general(q, kc, (((1,), (1,)), ((), ())), preferred_element_type=F32)


def _attn_bounded_kernel(q_ref, k_ref, vt_ref, o_ref, *, key_chunk, unroll):
    tq = q_ref.shape[0]
    L = k_ref.shape[0]
    hd = HEAD_DIM
    qs = [q_ref[:, g * hd:(g + 1) * hd] for g in range(GQA_GROUP)]

    def step(kc, vtc, carry):
        out = []
        for g, (acc, den) in enumerate(carry):
            p = jnp.exp2(_scores(kc, qs[g]))
            den = den + p.reshape(-1, 8, tq).sum(axis=0)
            out.append((acc + jnp.dot(vtc, p.astype(BF16), preferred_element_type=F32), den))
        return tuple(out)

    n_full = L // key_chunk
    tail = L - n_full * key_chunk
    carry = tuple((jnp.zeros((hd, tq), F32), jnp.zeros((8, tq), F32)) for _ in range(GQA_GROUP))
    if tail:
        carry = step(k_ref[pl.ds(n_full * key_chunk, tail), :], vt_ref[:, pl.ds(n_full * key_chunk, tail)], carry)

    def body(j, carry):
        k0 = pl.multiple_of(j * key_chunk, key_chunk)
        return step(k_ref[pl.ds(k0, key_chunk), :], vt_ref[:, pl.ds(k0, key_chunk)], carry)

    carry = lax.fori_loop(0, n_full, body, carry, unroll=unroll)
    for g in range(GQA_GROUP):
        acc, den = carry[g]
        out_t = acc * (1.0 / jnp.sum(den, axis=0, keepdims=True))
        o_ref[:, g * hd:(g + 1) * hd] = out_t.T.astype(BF16)


def _attn_online_kernel(q_ref, k_ref, v_ref, o_ref, *, key_chunk, unroll):
    tq = q_ref.shape[0]
    L = k_ref.shape[0]
    hd = HEAD_DIM
    qs = [q_ref[:, g * hd:(g + 1) * hd] for g in range(GQA_GROUP)]

    def step(kc, vc, carry):
        out = []
        for g in range(GQA_GROUP):
            m, l, acc = carry[g]
            s = _scores(qs[g], kc)
            m_new = jnp.maximum(m, jnp.max(s, axis=-1, keepdims=True))
            alpha = jnp.exp2(m - m_new)
            p = jnp.exp2(s - m_new)
            l = alpha * l + jnp.sum(p, axis=-1, keepdims=True)
            acc = alpha * acc + jnp.dot(p.astype(BF16), vc, preferred_element_type=F32)
            out.append((m_new, l, acc))
        return tuple(out)

    n_full = L // key_chunk
    tail = L - n_full * key_chunk
    carry = tuple((jnp.full((tq, 1), -jnp.inf, F32), jnp.zeros((tq, 1), F32), jnp.zeros((tq, hd), F32))
                  for _ in range(GQA_GROUP))
    if tail:
        carry = step(k_ref[pl.ds(n_full * key_chunk, tail), :], v_ref[pl.ds(n_full * key_chunk, tail), :],
                     carry)

    def body(j, carry):
        k0 = pl.multiple_of(j * key_chunk, key_chunk)
        return step(k_ref[pl.ds(k0, key_chunk), :], v_ref[pl.ds(k0, key_chunk), :], carry)

    carry = lax.fori_loop(0, n_full, body, carry, unroll=unroll)
    for g in range(GQA_GROUP):
        _, l, acc = carry[g]
        o_ref[:, g * hd:(g + 1) * hd] = (acc * (1.0 / l)).astype(BF16)


def _attention(q, k, v, *, bounded):
    B, L, nq = q.shape
    gw = GQA_GROUP * HEAD_DIM
    q_tile, key_chunk, unroll = FAST_ATTN_TILES if bounded else ONLINE_ATTN_TILES
    q_spec = pl.BlockSpec((None, q_tile, gw), lambda b, kh, i: (b, i, kh))
    kv_spec = pl.BlockSpec((None, L, HEAD_DIM), lambda b, kh, i: (b, 0, kh))
    if bounded:
        v_in = v.reshape(B, L, N_KV_HEADS, HEAD_DIM).transpose(0, 2, 3, 1)
        v_spec = pl.BlockSpec((None, None, HEAD_DIM, L), lambda b, kh, i: (b, kh, 0, 0))
    else:
        v_in, v_spec = v, kv_spec
    return pl.pallas_call(
        functools.partial(_attn_bounded_kernel if bounded else _attn_online_kernel, key_chunk=key_chunk,
                          unroll=unroll),
        out_shape=jax.ShapeDtypeStruct((B, L, nq), BF16),
        grid=(B, N_KV_HEADS, L // q_tile),
        in_specs=[q_spec, kv_spec, v_spec],
        out_specs=q_spec,
        compiler_params=_params("parallel", "parallel", "parallel"),
        name="attention_bounded" if bounded else "attention_online",
    )(q, k, v_in)


def _ffn_kernel(*refs, with_mixer_out):
    if with_mixer_out:
        z_ref, w_mix_ref, g_mix_ref, h_ref, g_pre_ref, w_in_ref, w_out_ref, g_post_ref, out_ref = refs
        m = jnp.dot(z_ref[...], w_mix_ref[...], preferred_element_type=F32)
        h = h_ref[...] + _rms(m, g_mix_ref[...])
    else:
        h_ref, g_pre_ref, w_in_ref, w_out_ref, g_post_ref, out_ref = refs
        h = h_ref[...]
    u = _rms(h, g_pre_ref[...]).astype(BF16)
    d_ff = w_out_ref.shape[0]
    acc = None
    step = FFN_CHUNK_TILES * MXU_DIM
    for c0, c1 in [(c, min(c + step, d_ff)) for c in range(0, d_ff, step)]:
        gate = jnp.dot(u, w_in_ref[:, c0:c1], preferred_element_type=F32)
        up = jnp.dot(u, w_in_ref[:, d_ff + c0:d_ff + c1], preferred_element_type=F32)
        act = (gate * _sigmoid(gate) * up).astype(BF16)
        part = jnp.dot(act, w_out_ref[c0:c1, :], preferred_element_type=F32)
        acc = part if acc is None else acc + part
    out_ref[...] = h + _rms(acc, g_post_ref[...])


def _ffn(h, g_pre, w_in, w_out, g_post, mixer_out=None, drop_leading=0):
    B, L, D = h.shape
    n_out = L - drop_leading
    if drop_leading:
        tile = FINAL_TILE
        flat = lambda a: a.reshape(B * L, a.shape[-1])
        tok = lambda width: pl.BlockSpec((pl.Element(tile), pl.Element(width)),
                                         lambda b, i: (pl.multiple_of(b * L + drop_leading + i * tile, BF16_ROWS), 0))
        out_shape = jax.ShapeDtypeStruct((B * n_out, D), F32)
        out_spec = pl.BlockSpec((tile, D), lambda b, i: (b * (n_out // tile) + i, 0))
    else:
        tile = TOKEN_TILE
        flat = lambda a: a
        tok = lambda width: _tok_spec(tile, width)
        out_shape = jax.ShapeDtypeStruct((B, L, D), F32)
        out_spec = _tok_spec(tile, D)
    args = [flat(h), g_pre, w_in, w_out, g_post]
    in_specs = [tok(D), _const_spec((1, D)), _const_spec(w_in.shape), _const_spec(w_out.shape),
                _const_spec((1, D))]
    if mixer_out is not None:
        z, w_mix, g_mix = mixer_out
        args = [flat(z), w_mix, g_mix] + args
        in_specs = [tok(z.shape[-1]), _const_spec(w_mix.shape), _const_spec((1, D))] + in_specs
    out = pl.pallas_call(
        functools.partial(_ffn_kernel, with_mixer_out=mixer_out is not None),
        out_shape=out_shape,
        grid=(B, n_out // tile),
        in_specs=in_specs,
        out_specs=out_spec,
        compiler_params=_params("parallel", "parallel"),
        name="ffn_after_attention" if mixer_out is not None else "ffn",
    )(*args)
    return out.reshape(B, n_out, D)


def _rope_tables(n_tokens):
    rows = n_tokens // GRID_W
    inv_freq = ROPE_THETA ** (-jnp.arange(0, 2 * ROPE_PAIRS, 2, dtype=F32) / (2 * ROPE_PAIRS))
    ang_r = jnp.arange(rows, dtype=F32)[:, None] * inv_freq
    ang_c = jnp.arange(GRID_W, dtype=F32)[:, None] * inv_freq
    ang_r = jnp.broadcast_to(ang_r[:, None, :], (rows, GRID_W, ROPE_PAIRS)).reshape(n_tokens, ROPE_PAIRS)
    ang_c = jnp.broadcast_to(ang_c[None, :, :], (rows, GRID_W, ROPE_PAIRS)).reshape(n_tokens, ROPE_PAIRS)
    ang = jnp.concatenate([ang_r, ang_c, ang_r, ang_c], axis=-1)
    ang = jnp.concatenate([jnp.zeros((N_META, HEAD_DIM), F32), ang], axis=0)
    sign = jnp.concatenate([-jnp.ones((2 * ROPE_PAIRS,), F32), jnp.ones((2 * ROPE_PAIRS,), F32)])
    return jnp.cos(ang), jnp.sin(ang) * sign


def _head_dim_order():
    p = ROPE_PAIRS
    return jnp.concatenate([jnp.arange(0, p), jnp.arange(2 * p, 3 * p), jnp.arange(p, 2 * p),
                            jnp.arange(3 * p, 4 * p)])


def _reorder_qk_columns(w_qkv):
    order = _head_dim_order()
    n_qk = (N_HEADS + N_KV_HEADS) * HEAD_DIM
    cols = (jnp.arange(n_qk).reshape(-1, HEAD_DIM)[:, :1] + order[None, :]).reshape(-1)
    return jnp.concatenate([w_qkv[:, cols], w_qkv[:, n_qk:]], axis=1)


def kernel(x, meta_tokens, norm_gains, lru_w_in, lru_conv_w, lru_conv_b, lru_gate_w, lru_gate_b,
           lru_lambda, lru_w_out, attn_w_qkv, attn_q_gain, attn_k_gain, attn_w_out, ffn_w_in,
           ffn_w_out):
    B, S, D = x.shape
    depth = norm_gains.shape[0]
    L = N_META + S
    h = (x, meta_tokens.astype(x.dtype))
    cos, sin = _rope_tables(S)
    gains = norm_gains.reshape(depth, 4, 1, D)
    for layer in range(depth):
        g = gains[layer]
        slot = layer // 2
        mixer_out = None
        if layer % 2 == 0:
            y, xb = _lru_in(h, g[0], lru_w_in[slot].astype(BF16), B, L)
            cw = lru_conv_w[slot]
            cb = lru_conv_b[slot].reshape(1, D)
            gw = _gate_weights(lru_gate_w[slot], lru_gate_b[slot])
            lam = lru_lambda[slot].reshape(2, 1, D)
            hf, xs = _lru_fwd(xb, cw, cb, gw[0], lam[0])
            h = _lru_bwd(xs, gw[1], lam[1], hf, y, h, lru_w_out[slot].astype(BF16), g[1])
        else:
            order = _head_dim_order()
            q, k, v = _qkv(h, g[0], _reorder_qk_columns(attn_w_qkv[slot]).astype(BF16),
                           attn_q_gain[slot][order].reshape(1, HEAD_DIM),
                           attn_k_gain[slot][order].reshape(1, HEAD_DIM), cos, sin)
            score_bound = (HEAD_DIM * Q_SCALE * jnp.max(jnp.abs(attn_q_gain[slot]))
                           * jnp.max(jnp.abs(attn_k_gain[slot])))
            o = lax.cond(score_bound <= SCORE_BOUND_MAX,
                         functools.partial(_attention, bounded=True),
                         functools.partial(_attention, bounded=False), q, k, v)
            mixer_out = (o, attn_w_out[slot].astype(BF16), g[1])
        h = _ffn(h, g[2], ffn_w_in[layer].astype(BF16), ffn_w_out[layer].astype(BF16), g[3], mixer_out,
                 drop_leading=N_META if layer == depth - 1 else 0)
    return h
```
